```python
import math
import jax, jax.numpy as jnp
from jax import lax
import numpy as np

D_MODEL = 1024
BATCH = 2
SEQ = 8192
DEPTH = 4

CHUNK = 64
N_A_LAYERS = DEPTH // 2
N_B_LAYERS = DEPTH - N_A_LAYERS
A_INNER = 2 * D_MODEL
A_HEADS = 4
A_HEAD_DIM = A_INNER // A_HEADS
A_CONV = 4
B_INNER = D_MODEL
B_HEADS = 16
B_HEAD_DIM = B_INNER // B_HEADS
B_PAST_CHUNKS = 8
B_PAST = B_PAST_CHUNKS * CHUNK
B_BAND = (B_PAST_CHUNKS + 1) * CHUNK
REL_CLIP = 128
DEEPNORM_ALPHA = (2.0 * DEPTH) ** 0.25
DEEPNORM_BETA = (8.0 * DEPTH) ** -0.25
LN_EPS = 1e-5
GN_EPS = 1e-6

kernel_name = "mlstm_yoco_chunked_relpos_attention_deepnorm"


def layer_norm(x, g, b):
    xf = x.astype(jnp.float32)
    mu = jnp.mean(xf, axis=-1, keepdims=True)
    var = jnp.mean(jnp.square(xf - mu), axis=-1, keepdims=True)
    return ((xf - mu) * lax.rsqrt(var + LN_EPS) * g.astype(jnp.float32) + b.astype(jnp.float32)).astype(x.dtype)


def causal_depthwise_conv(x, w, b):
    k_w = w.shape[0]
    s = x.shape[1]
    xp = jnp.pad(x, ((0, 0), (k_w - 1, 0), (0, 0)))
    out = xp[:, 0:s] * w[0]
    for tap in range(1, k_w):
        out = out + xp[:, tap:tap + s] * w[tap]
    return out + b


def mlstm_chunkwise(q, k, v, i_pre, log_f):
    bsz, s, h, dh = q.shape
    nc = s // CHUNK

    def to_chunks(t):
        t = t.reshape((bsz, nc, CHUNK, h) + t.shape[3:])
        return jnp.moveaxis(t, (1, 3), (0, 2))

    xs = (to_chunks(q), to_chunks(k), to_chunks(v), to_chunks(i_pre), to_chunks(log_f))
    causal = jnp.tril(jnp.ones((CHUNK, CHUNK), dtype=bool))

    def step(carry, inp):
        c_mat, n_vec, m = carry
        q_, k_, v_, i_, lf = inp
        bcum = jnp.cumsum(lf, axis=-1)
        dmat = bcum[..., :, None] - bcum[..., None, :] + i_[..., None, :]
        dmat = jnp.where(causal, dmat, -jnp.inf)
        m_inter = bcum + m[..., None]
        m_t = jnp.maximum(m_inter, jnp.max(dmat, axis=-1))
        w_intra = jnp.exp(dmat - m_t[..., None])
        w_inter = jnp.exp(m_inter - m_t)
        sw = jnp.einsum('bhtd,bhsd->bhts', q_, k_) * w_intra
        num = jnp.einsum('bhts,bhsd->bhtd', sw, v_) + w_inter[..., None] * jnp.einsum('bhtk,bhkd->bhtd', q_, c_mat)
        den = jnp.sum(sw, axis=-1) + w_inter * jnp.einsum('bhtk,bhk->bht', q_, n_vec)
        den = jnp.maximum(jnp.abs(den), jnp.exp(-m_t))
        h_out = num / den[..., None]
        b_last = bcum[..., -1]
        g = b_last[..., None] - bcum + i_
        m_new = jnp.maximum(b_last + m, jnp.max(g, axis=-1))
        decay = jnp.exp(b_last + m - m_new)
        wg = jnp.exp(g - m_new[..., None])
        c_new = decay[..., None, None] * c_mat + jnp.einsum('bhs,bhsk,bhsd->bhkd', wg, k_, v_)
        n_new = decay[..., None] * n_vec + jnp.einsum('bhs,bhsk->bhk', wg, k_)
        return (c_new, n_new, m_new), h_out

    init = (jnp.zeros((bsz, h, dh, dh), jnp.float32),
            jnp.zeros((bsz, h, dh), jnp.float32),
            jnp.zeros((bsz, h), jnp.float32))
    _, hs = lax.scan(step, init, xs)
    return jnp.moveaxis(hs, (0, 2), (1, 3)).reshape(bsz, s, h, dh)


def mlstm_layer(x, w_in, b_gate, conv_w, conv_b, w_q, w_k, w_v, gn_w, skip, w_out):
    bsz, s, _ = x.shape
    u = x @ w_in
    xm, z, o_pre, gates = jnp.split(u, [A_INNER, 2 * A_INNER, 3 * A_INNER], axis=-1)
    gates = (gates + b_gate).astype(jnp.float32)
    i_pre = gates[..., :A_HEADS]
    log_f = jax.nn.log_sigmoid(gates[..., A_HEADS:])
    xc = jax.nn.silu(causal_depthwise_conv(xm, conv_w, conv_b))
    heads = lambda t: t.reshape(bsz, s, A_HEADS, A_HEAD_DIM)
    q = jnp.einsum('bshd,hde->bshe', heads(xc), w_q)
    k = jnp.einsum('bshd,hde->bshe', heads(xc), w_k) * (A_HEAD_DIM ** -0.5)
    v = jnp.einsum('bshd,hde->bshe', heads(xm), w_v)
    h = mlstm_chunkwise(q.astype(jnp.float32), k.astype(jnp.float32), v.astype(jnp.float32), i_pre, log_f)
    mu = jnp.mean(h, axis=-1, keepdims=True)
    var = jnp.mean(jnp.square(h - mu), axis=-1, keepdims=True)
    hn = ((h - mu) * lax.rsqrt(var + GN_EPS)).reshape(bsz, s, A_INNER).astype(x.dtype) * gn_w
    h = jax.nn.sigmoid(o_pre) * hn + skip * xc
    return (h * jax.nn.silu(z)) @ w_out


def shared_band_kv(x, kv_w):
    bsz, s, _ = x.shape
    k, v = jnp.split(x @ kv_w, 2, axis=-1)

    def prep(t):
        t = t.reshape(bsz, s, B_HEADS, B_HEAD_DIM).transpose(0, 2, 1, 3)
        return jnp.pad(t, ((0, 0), (0, 0), (B_PAST, 0), (0, 0)))

    return prep(k), prep(v)


def chunk_attn_layer(x, k_pad, v_pad, w_in, rel_bias, w_out):
    bsz, s, _ = x.shape
    nc = s // CHUNK
    q, g = jnp.split(x @ w_in, 2, axis=-1)
    q = q.reshape(bsz, s, B_HEADS, B_HEAD_DIM).transpose(0, 2, 1, 3) * (B_HEAD_DIM ** -0.5)
    rel = jnp.arange(CHUNK)[:, None] + B_PAST - jnp.arange(B_BAND)[None, :]
    bias = rel_bias[:, jnp.clip(rel, -REL_CLIP, REL_CLIP) + REL_CLIP].astype(jnp.float32)
    key_off = jnp.arange(B_BAND) - B_PAST

    def one_chunk(c):
        start = c * CHUNK
        q_c = lax.dynamic_slice_in_dim(q, start, CHUNK, axis=2)
        k_c = lax.dynamic_slice_in_dim(k_pad, start, B_BAND, axis=2)
        v_c = lax.dynamic_slice_in_dim(v_pad, start, B_BAND, axis=2)
        sc = jnp.einsum('bhqd,bhkd->bhqk', q_c, k_c).astype(jnp.float32) + bias
        valid = (start + key_off) >= 0
        sc = jnp.where(valid, sc, -jnp.inf)
        p = jax.nn.softmax(sc, axis=-1)
        return jnp.einsum('bhqk,bhkd->bhqd', p.astype(v_c.dtype), v_c)

    o = lax.map(one_chunk, jnp.arange(nc))
    o = o.transpose(1, 0, 3, 2, 4).reshape(bsz, s, B_INNER)
    return (o * jax.nn.silu(g)) @ w_out


def setup_inputs(seed: int = 0) -> dict:
    key = jax.random.key(seed)
    ks = jax.random.split(key, 24)
    f32 = jnp.float32
    nrm = lambda k, shape, scale: jax.random.normal(k, shape, f32) * scale
    x = jax.random.normal(ks[0], (BATCH, SEQ, D_MODEL), f32)
    a_w_in = nrm(ks[1], (N_A_LAYERS, D_MODEL, 3 * A_INNER + 2 * A_HEADS), D_MODEL ** -0.5)
    i_bias = nrm(ks[2], (N_A_LAYERS, A_HEADS), 0.1)
    f_bias = jnp.linspace(3.0, 6.0, A_HEADS, dtype=f32)[None, :] + nrm(ks[3], (N_A_LAYERS, A_HEADS), 0.1)
    a_b_gate = jnp.concatenate([i_bias, f_bias], axis=-1)
    a_conv_w = nrm(ks[4], (N_A_LAYERS, A_CONV, A_INNER), A_CONV ** -0.5)
    a_conv_b = nrm(ks[5], (N_A_LAYERS, A_INNER), 0.02)
    a_w_q = nrm(ks[6], (N_A_LAYERS, A_HEADS, A_HEAD_DIM, A_HEAD_DIM), A_HEAD_DIM ** -0.5)
    a_w_k = nrm(ks[7], (N_A_LAYERS, A_HEADS, A_HEAD_DIM, A_HEAD_DIM), A_HEAD_DIM ** -0.5)
    a_w_v = nrm(ks[8], (N_A_LAYERS, A_HEADS, A_HEAD_DIM, A_HEAD_DIM), A_HEAD_DIM ** -0.5)
    a_gn_w = 1.0 + nrm(ks[9], (N_A_LAYERS, A_INNER), 0.02)
    a_skip = 1.0 + nrm(ks[10], (N_A_LAYERS, A_INNER), 0.02)
    a_w_out = nrm(ks[11], (N_A_LAYERS, A_INNER, D_MODEL), DEEPNORM_BETA * A_INNER ** -0.5)
    a_ln_g = 1.0 + nrm(ks[12], (N_A_LAYERS, D_MODEL), 0.02)
    a_ln_b = nrm(ks[13], (N_A_LAYERS, D_MODEL), 0.02)
    kv_w = nrm(ks[14], (D_MODEL, 2 * B_INNER), D_MODEL ** -0.5)
    b_w_in = nrm(ks[15], (N_B_LAYERS, D_MODEL, 2 * B_INNER), D_MODEL ** -0.5)
    b_rel_bias = nrm(ks[16], (N_B_LAYERS, B_HEADS, 2 * REL_CLIP + 1), 0.1)
    b_w_out = nrm(ks[17], (N_B_LAYERS, B_INNER, D_MODEL), DEEPNORM_BETA * B_INNER ** -0.5)
    b_ln_g = 1.0 + nrm(ks[18], (N_B_LAYERS, D_MODEL), 0.02)
    b_ln_b = nrm(ks[19], (N_B_LAYERS, D_MODEL), 0.02)
    return {"x": x, "a_w_in": a_w_in, "a_b_gate": a_b_gate, "a_conv_w": a_conv_w,
            "a_conv_b": a_conv_b, "a_w_q": a_w_q, "a_w_k": a_w_k, "a_w_v": a_w_v,
            "a_gn_w": a_gn_w, "a_skip": a_skip, "a_w_out": a_w_out, "a_ln_g": a_ln_g,
            "a_ln_b": a_ln_b, "kv_w": kv_w, "b_w_in": b_w_in, "b_rel_bias": b_rel_bias,
            "b_w_out": b_w_out, "b_ln_g": b_ln_g, "b_ln_b": b_ln_b}


def reference(x, a_w_in, a_b_gate, a_conv_w, a_conv_b, a_w_q, a_w_k, a_w_v, a_gn_w, a_skip,
              a_w_out, a_ln_g, a_ln_b, kv_w, b_w_in, b_rel_bias, b_w_out, b_ln_g, b_ln_b):
    k_pad = None
    v_pad = None
    for layer in range(DEPTH):
        if layer < N_A_LAYERS:
            l = layer
            y = mlstm_layer(x, a_w_in[l], a_b_gate[l], a_conv_w[l], a_conv_b[l], a_w_q[l],
                            a_w_k[l], a_w_v[l], a_gn_w[l], a_skip[l], a_w_out[l])
            x = layer_norm(DEEPNORM_ALPHA * x + y, a_ln_g[l], a_ln_b[l])
            if layer == N_A_LAYERS - 1:
                k_pad, v_pad = shared_band_kv(x, kv_w)
        else:
            l = layer - N_A_LAYERS
            y = chunk_attn_layer(x, k_pad, v_pad, b_w_in[l], b_rel_bias[l], b_w_out[l])
            x = layer_norm(DEEPNORM_ALPHA * x + y, b_ln_g[l], b_ln_b[l])
    return x
```

```python
import functools

import jax
import jax.numpy as jnp
from jax import lax
from jax.experimental import pallas as pl
from jax.experimental.pallas import tpu as pltpu

F32 = jnp.float32
BF16 = jnp.bfloat16

LN_EPS = 1e-5
GN_EPS = 1e-6
ATTN_CHUNK = 64
ATTN_PAST_CHUNKS = 8
LANES = 128
GATE_PAD = 128
VMEM_LIMIT = 48 * 1024 * 1024

MLSTM_CHUNK = 256
ATTN_QBLK = 256
ATTN_STEP_ROWS = 1024


def _params(*sem):
    return pltpu.CompilerParams(dimension_semantics=sem, vmem_limit_bytes=VMEM_LIMIT)


def _sigmoid(x):
    return 1.0 / (1.0 + jnp.exp(-x))


def _log_sigmoid(x):
    return jnp.minimum(x, 0.0) - jnp.log1p(jnp.exp(-jnp.abs(x)))


def _layer_norm(r, g, b):
    mu = jnp.mean(r, axis=-1, keepdims=True)
    d = r - mu
    var = jnp.mean(d * d, axis=-1, keepdims=True)
    return d * lax.rsqrt(var + LN_EPS) * g + b


def _mm_kernel(x_ref, w_ref, b_ref, o_ref):
    acc = jnp.dot(x_ref[...].astype(BF16), w_ref[...], preferred_element_type=F32)
    o_ref[...] = (acc + b_ref[...]).astype(o_ref.dtype)


def _matmul(x, w, *, bm, bn, out_dtype, name, bias=None):
    m, k = x.shape
    n = w.shape[1]
    bm, bn = min(bm, m), min(bn, n)
    while n % bn:
        bn -= LANES
    assert m % bm == 0 and bn > 0
    if bias is None:
        bias = jnp.zeros((n,), F32)
    return pl.pallas_call(
        _mm_kernel,
        grid=(m // bm, n // bn),
        in_specs=[pl.BlockSpec((bm, k), lambda i, j: (i, 0)),
                  pl.BlockSpec((k, bn), lambda i, j: (0, j)),
                  pl.BlockSpec((1, bn), lambda i, j: (0, j))],
        out_specs=pl.BlockSpec((bm, bn), lambda i, j: (i, j)),
        out_shape=jax.ShapeDtypeStruct((m, n), out_dtype),
        compiler_params=_params("parallel", "arbitrary"),
        name=name,
    )(x, w, bias.reshape(1, n).astype(F32))


def _conv_qkv_kernel(xm_ref, cw_ref, cb_ref, wq_ref, wk_ref, wv_ref,
                     xc_ref, q_ref, k_ref, v_ref, xbuf, *, tm, tiles_per_seq, taps, kscale):
    i = pl.program_id(1)
    halo = 8

    @pl.when(i % tiles_per_seq == 0)
    def _():
        xbuf[0:halo, :] = jnp.zeros((halo, xbuf.shape[1]), F32)

    @pl.when(i % tiles_per_seq != 0)
    def _():
        xbuf[0:halo, :] = xbuf[tm:tm + halo, :]

    xm = xm_ref[...]
    xbuf[halo:halo + tm, :] = xm.astype(F32)
    acc = cb_ref[...]
    for tap in range(taps):
        off = halo - (taps - 1) + tap
        acc = acc + xbuf[off:off + tm, :] * cw_ref[tap:tap + 1, :]
    xc = (acc * _sigmoid(acc)).astype(BF16)
    xc_ref[...] = xc
    q_ref[...] = jnp.dot(xc, wq_ref[...], preferred_element_type=F32).astype(BF16)
    k_ref[...] = (jnp.dot(xc, wk_ref[...], preferred_element_type=F32) * kscale).astype(BF16)
    v_ref[...] = jnp.dot(xm, wv_ref[...], preferred_element_type=F32).astype(BF16)


def _conv_qkv(u, conv_w, conv_b, w_q, w_k, w_v, *, seq, tm):
    t = u.shape[0]
    heads, dh, _ = w_q.shape
    inner = heads * dh
    taps = conv_w.shape[0]
    tm = min(tm, seq)
    assert seq % tm == 0 and taps - 1 <= 8
    out = jax.ShapeDtypeStruct((t, inner), BF16)
    row_blk = pl.BlockSpec((tm, dh), lambda h, i: (i, h))
    w_blk = pl.BlockSpec((None, dh, dh), lambda h, i: (h, 0, 0))
    kernel = functools.partial(_conv_qkv_kernel, tm=tm, tiles_per_seq=seq // tm, taps=taps,
                               kscale=float(dh) ** -0.5)
    return pl.pallas_call(
        kernel,
        grid=(heads, t // tm),
        in_specs=[row_blk,
                  pl.BlockSpec((taps, dh), lambda h, i: (0, h)),
                  pl.BlockSpec((1, dh), lambda h, i: (0, h)),
                  w_blk, w_blk, w_blk],
        out_specs=[row_blk, row_blk, row_blk, row_blk],
        out_shape=[out, out, out, out],
        scratch_shapes=[pltpu.VMEM((tm + 16, dh), F32)],
        compiler_params=_params("arbitrary", "arbitrary"),
        name="mlstm_conv_qkv",
    )(u, conv_w, conv_b.reshape(1, inner), w_q, w_k, w_v)


def _mlstm_kernel(q_ref, k_ref, v_ref, gcol_ref, grow_ref, h_ref, c_scr, n_scr, m_scr,
                  *, chunk, heads, dh):
    @pl.when(pl.program_id(1) == 0)
    def _():
        c_scr[...] = jnp.zeros(c_scr.shape, F32)
        n_scr[...] = jnp.zeros(n_scr.shape, F32)
        m_scr[...] = jnp.zeros(m_scr.shape, F32)

    gcol = gcol_ref[...]
    grow = grow_ref[0]
    t_id = lax.broadcasted_iota(jnp.int32, (chunk, chunk), 0)
    s_id = lax.broadcasted_iota(jnp.int32, (chunk, chunk), 1)
    causal = s_id <= t_id

    for h in range(heads):
        i_col = gcol[:, h:h + 1]
        i_row = grow[h:h + 1, :]
        lf_col = _log_sigmoid(gcol[:, heads + h:heads + h + 1])
        lf_row = _log_sigmoid(grow[heads + h:heads + h + 1, :])
        bcum_col = jnp.sum(jnp.where(causal, lf_row, 0.0), axis=1, keepdims=True)
        bcum_row = jnp.sum(jnp.where(t_id <= s_id, lf_col, 0.0), axis=0, keepdims=True)
        m_prev = m_scr[h, 0:1, 0:1]
        dmat = jnp.where(causal, bcum_col - bcum_row + i_row, -jnp.inf)
        m_inter = bcum_col + m_prev
        m_t = jnp.maximum(m_inter, jnp.max(dmat, axis=1, keepdims=True))
        w_intra = jnp.exp(dmat - m_t)
        w_inter = jnp.exp(m_inter - m_t)

        sl = slice(h * dh, (h + 1) * dh)
        q = q_ref[:, sl]
        k = k_ref[:, sl]
        v = v_ref[:, sl]
        s = lax.dot_general(q, k, (((1,), (1,)), ((), ())), preferred_element_type=F32)
        sw = s * w_intra
        c_prev = c_scr[h]
        n_prev = n_scr[h]
        num = (jnp.dot(sw.astype(BF16), v, preferred_element_type=F32)
               + w_inter * jnp.dot(q, c_prev.astype(BF16), preferred_element_type=F32))
        qn = jnp.sum(q.astype(F32) * n_prev, axis=1, keepdims=True)
        den = jnp.sum(sw, axis=1, keepdims=True) + w_inter * qn
        den = jnp.maximum(jnp.abs(den), jnp.exp(-m_t))
        hh = num / den
        mu = jnp.mean(hh, axis=1, keepdims=True)
        d = hh - mu
        var = jnp.mean(d * d, axis=1, keepdims=True)
        h_ref[:, sl] = (d * lax.rsqrt(var + GN_EPS)).astype(h_ref.dtype)

        b_last = bcum_col[chunk - 1:chunk, :]
        g_col = b_last - bcum_col + i_col
        g_row = b_last - bcum_row + i_row
        m_new = jnp.maximum(b_last + m_prev, jnp.max(g_row, axis=1, keepdims=True))
        decay = jnp.exp(b_last + m_prev - m_new)
        kw = k.astype(F32) * jnp.exp(g_col - m_new)
        c_scr[h] = decay * c_prev + lax.dot_general(
            kw.astype(BF16), v, (((0,), (0,)), ((), ())), preferred_element_type=F32)
        n_scr[h] = decay * n_prev + jnp.sum(kw, axis=0, keepdims=True)
        m_scr[h] = jnp.broadcast_to(m_new, m_scr.shape[1:])


def _mlstm(q, k, v, gates, *, batch, seq, heads, chunk):
    t, inner = q.shape
    dh = inner // heads
    chunk = min(chunk, seq)
    assert seq % chunk == 0
    nc = seq // chunk
    g = gates[:, :2 * heads]
    g_rows = g.reshape(batch * nc, chunk, 2 * heads).transpose(0, 2, 1)
    blk = pl.BlockSpec((chunk, inner), lambda b, c: (b * nc + c, 0))
    kernel = functools.partial(_mlstm_kernel, chunk=chunk, heads=heads, dh=dh)
    return pl.pallas_call(
        kernel,
        grid=(batch, nc),
        in_specs=[blk, blk, blk,
                  pl.BlockSpec((chunk, 2 * heads), lambda b, c: (b * nc + c, 0)),
                  pl.BlockSpec((1, 2 * heads, chunk), lambda b, c: (b * nc + c, 0, 0))],
        out_specs=blk,
        out_shape=jax.ShapeDtypeStruct((t, inner), BF16),
        scratch_shapes=[pltpu.VMEM((heads, dh, dh), F32),
                        pltpu.VMEM((heads, 1, dh), F32),
                        pltpu.VMEM((heads, 8, LANES), F32)],
        compiler_params=_params("arbitrary", "arbitrary"),
        name="mlstm_scan",
    )(q, k, v, g, g_rows)


def _mlstm_out_kernel(hn_ref, o_ref, z_ref, xc_ref, x_ref, gnw_ref, skip_ref, w_ref, g_ref, b_ref,
                      y_ref, *, alpha):
    hn = hn_ref[...].astype(F32) * gnw_ref[...]
    hh = _sigmoid(o_ref[...].astype(F32)) * hn + skip_ref[...] * xc_ref[...].astype(F32)
    z = z_ref[...].astype(F32)
    hz = hh * (z * _sigmoid(z))
    y = jnp.dot(hz.astype(BF16), w_ref[...], preferred_element_type=F32)
    y_ref[...] = _layer_norm(alpha * x_ref[...] + y, g_ref[...], b_ref[...])


def _mlstm_out(hn, u, xc, x, gn_w, skip, w_out, ln_g, ln_b, *, alpha, tm):
    t, inner = hn.shape
    d = x.shape[1]
    tm = min(tm, t)
    row = lambda cb: pl.BlockSpec((tm, inner), lambda i, cb=cb: (i, cb))
    vec = lambda n: pl.BlockSpec((1, n), lambda i: (0, 0))
    return pl.pallas_call(
        functools.partial(_mlstm_out_kernel, alpha=alpha),
        grid=(t // tm,),
        in_specs=[row(0), row(2), row(1), row(0),
                  pl.BlockSpec((tm, d), lambda i: (i, 0)),
                  vec(inner), vec(inner),
                  pl.BlockSpec((inner, d), lambda i: (0, 0)),
                  vec(d), vec(d)],
        out_specs=pl.BlockSpec((tm, d), lambda i: (i, 0)),
        out_shape=jax.ShapeDtypeStruct((t, d), F32),
        compiler_params=_params("parallel"),
        name="mlstm_out_ln",
    )(hn, u, u, xc, x, gn_w.reshape(1, inner), skip.reshape(1, inner), w_out,
      ln_g.reshape(1, d), ln_b.reshape(1, d))


def _mlstm_layer(x, p, *, batch, seq, alpha):
    heads, dh, _ = p["w_q"].shape
    inner = heads * dh
    u = _matmul(x, p["w_in"][:, :3 * inner], bm=1024, bn=1024, out_dtype=BF16, name="mlstm_in_proj")
    gates = _matmul(x, p["w_gate"], bm=1024, bn=GATE_PAD, out_dtype=F32, name="mlstm_gate_proj",
                    bias=p["b_gate"])
    xc, q, k, v = _conv_qkv(u, p["conv_w"], p["conv_b"], p["w_q"], p["w_k"], p["w_v"],
                            seq=seq, tm=1024)
    hn = _mlstm(q, k, v, gates, batch=batch, seq=seq, heads=heads, chunk=MLSTM_CHUNK)
    return _mlstm_out(hn, u, xc, x, p["gn_w"], p["skip"], p["w_out"], p["ln_g"], p["ln_b"],
                      alpha=alpha, tm=512)


def _bias_table_kernel(r_ref, o_ref, *, qblk, width, past):
    rolled = pltpu.roll(jnp.broadcast_to(r_ref[0], (qblk, r_ref.shape[2])), 0, 1,
                        stride=1, stride_axis=0)
    bias = rolled[:, :width]
    cq = lax.broadcasted_iota(jnp.int32, (qblk, width), 0) // ATTN_CHUNK
    col = lax.broadcasted_iota(jnp.int32, (qblk, width), 1)
    ck = col // ATTN_CHUNK
    band = (ck >= cq) & (ck <= cq + ATTN_PAST_CHUNKS)
    for variant in range(o_ref.shape[0]):
        first_valid = 0 if variant == 0 else past - (variant - 1) * qblk
        o_ref[variant, 0] = jnp.where(band & (col >= first_valid), bias, -jnp.inf)


def _bias_tables(rel_bias, *, qblk, past):
    heads, nrel = rel_bias.shape
    clip = (nrel - 1) // 2
    width = past + qblk
    rlen = 1 << (width + qblk - 1).bit_length()
    assert past >= clip and width - past >= 0
    far_past = jnp.broadcast_to(rel_bias[:, nrel - 1:], (heads, past - clip))
    future = jnp.broadcast_to(rel_bias[:, :1], (heads, width - (past + clip)))
    wrapped = jnp.broadcast_to(rel_bias[:, nrel - 1:], (heads, rlen - width - 1))
    r = jnp.concatenate([far_past, rel_bias[:, ::-1], future, wrapped], axis=1)
    assert r.shape[1] == rlen
    n_var = past // qblk + 1
    return pl.pallas_call(
        functools.partial(_bias_table_kernel, qblk=qblk, width=width, past=past),
        grid=(heads,),
        in_specs=[pl.BlockSpec((1, 1, rlen), lambda h: (h, 0, 0))],
        out_specs=pl.BlockSpec((n_var, 1, qblk, width), lambda h: (0, h, 0, 0)),
        out_shape=jax.ShapeDtypeStruct((n_var, heads, qblk, width), F32),
        compiler_params=_params("parallel"),
        name="attn_bias_table",
    )(r.reshape(heads, 1, rlen).astype(F32))


def _attn_kernel(q_ref, kp_ref, kc_ref, vp_ref, vc_ref, bias_ref, o_ref,
                 *, qblk, past, steps_per_seq, dh):
    first = (pl.program_id(1) % steps_per_seq) == 0
    rows = q_ref.shape[0]
    n_past = past // qblk
    n_kblk = n_past + 1
    lane = lax.broadcasted_iota(jnp.int32, (qblk, LANES), 1)
    low_half = lane < dh

    def kv_block(prev_ref, cur_ref, j):
        if j < n_past:
            return prev_ref[j * qblk:(j + 1) * qblk, :]
        return cur_ref[(j - n_past) * qblk:(j - n_past + 1) * qblk, :]

    for t in range(rows // qblk):
        q2 = q_ref[t * qblk:(t + 1) * qblk, :]
        q2 = (q2.astype(F32) * (float(dh) ** -0.5)).astype(BF16)
        kb = [kv_block(kp_ref, kc_ref, t + j) for j in range(n_kblk)]
        vb = [kv_block(vp_ref, vc_ref, t + j) for j in range(n_kblk)]
        variant = jnp.where(first, t + 1, 0) if t < n_past else 0
        outs = []
        for e in range(2):
            qe = jnp.where(low_half if e == 0 else jnp.logical_not(low_half), q2, jnp.zeros_like(q2))
            s = jnp.concatenate(
                [lax.dot_general(qe, kb[j], (((1,), (1,)), ((), ())), preferred_element_type=F32)
                 for j in range(n_kblk)], axis=1)
            s = s + bias_ref[variant, e]
            m = jnp.max(s, axis=1, keepdims=True)
            p = jnp.exp(s - m)
            l = jnp.sum(p, axis=1, keepdims=True)
            pb = p.astype(BF16)
            o = jnp.dot(pb[:, 0:qblk], vb[0], preferred_element_type=F32)
            for j in range(1, n_kblk):
                o = o + jnp.dot(pb[:, j * qblk:(j + 1) * qblk], vb[j], preferred_element_type=F32)
            outs.append(o / l)
        o_ref[t * qblk:(t + 1) * qblk, :] = jnp.where(low_half, outs[0], outs[1]).astype(o_ref.dtype)


def _attention(qg, kv, tables, *, seq, heads, dh):
    t = qg.shape[0]
    inner = heads * dh
    assert 2 * dh == LANES and heads % 2 == 0
    pairs = heads // 2
    qblk = ATTN_QBLK
    past = ATTN_PAST_CHUNKS * ATTN_CHUNK
    rows = min(ATTN_STEP_ROWS, seq)
    assert seq % rows == 0 and rows % past == 0 and past % qblk == 0 and qblk % ATTN_CHUNK == 0
    steps_per_seq = seq // rows
    ratio = rows // past
    v_off = inner // LANES
    n_var = tables.shape[0]
    width = tables.shape[3]
    prev_idx = lambda i: jnp.maximum(i * ratio - 1, 0)
    kernel = functools.partial(_attn_kernel, qblk=qblk, past=past, steps_per_seq=steps_per_seq, dh=dh)
    return pl.pallas_call(
        kernel,
        grid=(pairs, t // rows),
        in_specs=[pl.BlockSpec((rows, LANES), lambda p, i: (i, p)),
                  pl.BlockSpec((past, LANES), lambda p, i: (prev_idx(i), p)),
                  pl.BlockSpec((rows, LANES), lambda p, i: (i, p)),
                  pl.BlockSpec((past, LANES), lambda p, i: (prev_idx(i), v_off + p)),
                  pl.BlockSpec((rows, LANES), lambda p, i: (i, v_off + p)),
                  pl.BlockSpec((n_var, 2, qblk, width), lambda p, i: (0, p, 0, 0))],
        out_specs=pl.BlockSpec((rows, LANES), lambda p, i: (i, p)),
        out_shape=jax.ShapeDtypeStruct((t, inner), BF16),
        compiler_params=_params("parallel", "arbitrary"),
        name="chunk_attention",
    )(qg, kv, kv, kv, kv, tables)


def _attn_out_kernel(o_ref, g_ref, x_ref, w_ref, lg_ref, lb_ref, y_ref, *, alpha):
    g = g_ref[...].astype(F32)
    og = o_ref[...].astype(F32) * (g * _sigmoid(g))
    y = jnp.dot(og.astype(BF16), w_ref[...], preferred_element_type=F32)
    y_ref[...] = _layer_norm(alpha * x_ref[...] + y, lg_ref[...], lb_ref[...])


def _attn_out(o, qg, x, w_out, ln_g, ln_b, *, alpha, tm):
    t, inner = o.shape
    d = x.shape[1]
    tm = min(tm, t)
    vec = pl.BlockSpec((1, d), lambda i: (0, 0))
    return pl.pallas_call(
        functools.partial(_attn_out_kernel, alpha=alpha),
        grid=(t // tm,),
        in_specs=[pl.BlockSpec((tm, inner), lambda i: (i, 0)),
                  pl.BlockSpec((tm, inner), lambda i: (i, 1)),
                  pl.BlockSpec((tm, d), lambda i: (i, 0)),
                  pl.BlockSpec((inner, d), lambda i: (0, 0)),
                  vec, vec],
        out_specs=pl.BlockSpec((tm, d), lambda i: (i, 0)),
        out_shape=jax.ShapeDtypeStruct((t, d), F32),
        compiler_params=_params("parallel"),
        name="attn_out_ln",
    )(o, qg, x, w_out, ln_g.reshape(1, d), ln_b.reshape(1, d))


def _attn_layer(x, kv, p, *, seq, alpha):
    heads = p["rel_bias"].shape[0]
    inner = p["w_out"].shape[0]
    dh = inner // heads
    qg = _matmul(x, p["w_in"], bm=1024, bn=1024, out_dtype=BF16, name="attn_in_proj")
    tables = _bias_tables(p["rel_bias"], qblk=ATTN_QBLK, past=ATTN_PAST_CHUNKS * ATTN_CHUNK)
    o = _attention(qg, kv, tables, seq=seq, heads=heads, dh=dh)
    return _attn_out(o, qg, x, p["w_out"], p["ln_g"], p["ln_b"], alpha=alpha, tm=512)


def kernel(x, a_w_in, a_b_gate, a_conv_w, a_conv_b, a_w_q, a_w_k, a_w_v, a_gn_w, a_skip, a_w_out,
           a_ln_g, a_ln_b, kv_w, b_w_in, b_rel_bias, b_w_out, b_ln_g, b_ln_b):
    batch, seq, d = x.shape
    n_a, n_b = a_w_in.shape[0], b_w_in.shape[0]
    alpha = (2.0 * (n_a + n_b)) ** 0.25
    heads = a_w_q.shape[1]
    inner = a_conv_w.shape[2]
    n_gate = 2 * heads
    h = x.reshape(batch * seq, d)
    for l in range(n_a):
        w_gate = jnp.pad(a_w_in[l][:, 3 * inner:], ((0, 0), (0, GATE_PAD - n_gate)))
        p = dict(w_in=a_w_in[l].astype(BF16), w_gate=w_gate.astype(BF16),
                 b_gate=jnp.pad(a_b_gate[l], (0, GATE_PAD - n_gate)),
                 conv_w=a_conv_w[l], conv_b=a_conv_b[l],
                 w_q=a_w_q[l].astype(BF16), w_k=a_w_k[l].astype(BF16), w_v=a_w_v[l].astype(BF16),
                 gn_w=a_gn_w[l], skip=a_skip[l], w_out=a_w_out[l].astype(BF16),
                 ln_g=a_ln_g[l], ln_b=a_ln_b[l])
        h = _mlstm_layer(h, p, batch=batch, seq=seq, alpha=alpha)
    kv = _matmul(h, kv_w.astype(BF16), bm=1024, bn=1024, out_dtype=BF16, name="shared_kv_proj")
    for l in range(n_b):
        p = dict(w_in=b_w_in[l].astype(BF16), rel_bias=b_rel_bias[l], w_out=b_w_out[l].astype(BF16),
                 ln_g=b_ln_g[l], ln_b=b_ln_b[l])
        h = _attn_layer(h, kv, p, seq=seq, alpha=alpha)
    return h.reshape(batch, seq, d)
```

```python
import functools

import jax
import jax.numpy as jnp
from jax import lax
from jax.experimental import pallas as pl
from jax.experimental.pallas import tpu as pltpu

F32 = jnp.float32
BF16 = jnp.bfloat16

LOG2E = 1.4426950408889634
LN_EPS = 1e-5
GN_EPS = 1e-6
ATTN_CHUNK = 64
ATTN_PAST_CHUNKS = 8
LANES = 128
GATE_PAD = 128
VMEM_LIMIT = 48 * 1024 * 1024

CONV_SUB_ROWS = 128
OUT_SUB_ROWS = 512
MLSTM_CHUNK = 256
ATTN_QBLK = 256
ATTN_STEP_ROWS = 1024


def _params(*sem):
    return pltpu.CompilerParams(dimension_semantics=sem, vmem_limit_bytes=VMEM_LIMIT)


def _sigmoid(x):
    return 0.5 * jnp.tanh(0.5 * x) + 0.5


def _silu(x):
    h = 0.5 * x
    return h * jnp.tanh(h) + h


def _log_sigmoid(x):
    return jnp.minimum(x, 0.0) - jnp.log1p(jnp.exp(-jnp.abs(x)))


def _split3(x):
    hi = x.astype(BF16)
    r1 = x - hi.astype(F32)
    mid = r1.astype(BF16)
    lo = (r1 - mid.astype(F32)).astype(BF16)
    return hi, mid, lo


def _layer_norm(r, g, b):
    mu = jnp.mean(r, axis=-1, keepdims=True)
    d = r - mu
    var = jnp.mean(d * d, axis=-1, keepdims=True)
    return d * lax.rsqrt(var + LN_EPS) * g + b


def _mm_kernel(x_ref, w_ref, b_ref, o_ref):
    acc = jnp.dot(x_ref[...].astype(BF16), w_ref[...], preferred_element_type=F32)
    o_ref[...] = (acc + b_ref[...]).astype(o_ref.dtype)


def _matmul(x, w, *, bm, bn, out_dtype, name, bias=None):
    m, k = x.shape
    n = w.shape[1]
    bm, bn = min(bm, m), min(bn, n)
    while n % bn:
        bn -= LANES
    assert m % bm == 0 and bn > 0
    if bias is None:
        bias = jnp.zeros((n,), F32)
    return pl.pallas_call(
        _mm_kernel,
        grid=(m // bm, n // bn),
        in_specs=[pl.BlockSpec((bm, k), lambda i, j: (i, 0)),
                  pl.BlockSpec((k, bn), lambda i, j: (0, j)),
                  pl.BlockSpec((1, bn), lambda i, j: (0, j))],
        out_specs=pl.BlockSpec((bm, bn), lambda i, j: (i, j)),
        out_shape=jax.ShapeDtypeStruct((m, n), out_dtype),
        compiler_params=_params("parallel", "arbitrary"),
        name=name,
    )(x, w, bias.reshape(1, n).astype(F32))


def _conv_qkv_kernel(xm_ref, cw_ref, cb_ref, wq_ref, wk_ref, wv_ref,
                     xc_ref, q_ref, k_ref, v_ref, xbuf, *, tm, sub, tiles_per_seq, taps, kscale):
    i = pl.program_id(1)
    halo = xbuf.shape[0]

    @pl.when(i % tiles_per_seq == 0)
    def _():
        xbuf[...] = jnp.zeros(xbuf.shape, F32)

    prev = xbuf[...]
    for r0 in range(0, tm, sub):
        rows = slice(r0, r0 + sub)
        xm = xm_ref[rows, :]
        x = xm.astype(F32)
        xe = jnp.concatenate([prev, x], axis=0)
        prev = x[sub - halo:sub, :]
        acc = cb_ref[...] + x * cw_ref[taps - 1:taps, :]
        for shift in range(1, taps):
            xs = pltpu.roll(xe, shift, 0)[halo:halo + sub, :]
            acc = acc + xs * cw_ref[taps - 1 - shift:taps - shift, :]
        xc = _silu(acc).astype(BF16)
        xc_ref[rows, :] = xc
        q_ref[rows, :] = jnp.dot(xc, wq_ref[...], preferred_element_type=F32).astype(BF16)
        k_ref[rows, :] = (jnp.dot(xc, wk_ref[...], preferred_element_type=F32) * kscale).astype(BF16)
        v_ref[rows, :] = jnp.dot(xm, wv_ref[...], preferred_element_type=F32).astype(BF16)
    xbuf[...] = prev


def _conv_qkv(u, conv_w, conv_b, w_q, w_k, w_v, *, seq, tm):
    t = u.shape[0]
    heads, dh, _ = w_q.shape
    inner = heads * dh
    taps = conv_w.shape[0]
    tm = min(tm, seq)
    assert seq % tm == 0 and taps - 1 <= 8
    out = jax.ShapeDtypeStruct((t, inner), BF16)
    row_blk = pl.BlockSpec((tm, dh), lambda h, i: (i, h))
    w_blk = pl.BlockSpec((None, dh, dh), lambda h, i: (h, 0, 0))
    sub = min(CONV_SUB_ROWS, tm)
    assert tm % sub == 0
    kernel = functools.partial(_conv_qkv_kernel, tm=tm, sub=sub, tiles_per_seq=seq // tm, taps=taps,
                               kscale=float(dh) ** -0.5)
    return pl.pallas_call(
        kernel,
        grid=(heads, t // tm),
        in_specs=[row_blk,
                  pl.BlockSpec((taps, dh), lambda h, i: (0, h)),
                  pl.BlockSpec((1, dh), lambda h, i: (0, h)),
                  w_blk, w_blk, w_blk],
        out_specs=[row_blk, row_blk, row_blk, row_blk],
        out_shape=[out, out, out, out],
        scratch_shapes=[pltpu.VMEM((8, dh), F32)],
        compiler_params=_params("arbitrary", "arbitrary"),
        name="mlstm_conv_qkv",
    )(u, conv_w, conv_b.reshape(1, inner), w_q, w_k, w_v)


def _mlstm_kernel(q_ref, k_ref, v_ref, g_ref, h_ref, c_scr, n_scr, m_scr, *, chunk, heads, dh):
    @pl.when(pl.program_id(1) == 0)
    def _():
        c_scr[...] = jnp.zeros(c_scr.shape, F32)
        n_scr[...] = jnp.zeros(n_scr.shape, F32)
        m_scr[...] = jnp.zeros(m_scr.shape, F32)

    t_id = lax.broadcasted_iota(jnp.int32, (chunk, chunk), 0)
    s_id = lax.broadcasted_iota(jnp.int32, (chunk, chunk), 1)
    causal = s_id <= t_id
    triu = jnp.where(t_id <= s_id, 1.0, 0.0).astype(BF16)

    grow = g_ref[0]
    i_rows = grow[:heads, :] * LOG2E
    lf_rows = _log_sigmoid(grow[heads:, :]) * LOG2E
    br = jnp.dot(jnp.concatenate(_split3(lf_rows), axis=0), triu, preferred_element_type=F32)
    bcum_rows = br[:heads] + br[heads:2 * heads] + br[2 * heads:]

    for h in range(heads):
        lf_row = lf_rows[h:h + 1, :]
        bcum_col = jnp.sum(jnp.where(causal, lf_row, 0.0), axis=1, keepdims=True)
        m_prev = m_scr[h, 0:1, 0:1]
        b_last = bcum_col[chunk - 1:chunk, :]
        r_row = i_rows[h:h + 1, :] - bcum_rows[h:h + 1, :]
        dmat = jnp.where(causal, bcum_col + r_row, -jnp.inf)
        m_inter = bcum_col + m_prev
        m_t = jnp.maximum(m_inter, jnp.max(dmat, axis=1, keepdims=True))
        w_intra = jnp.exp2(dmat - m_t)
        w_inter = jnp.exp2(m_inter - m_t)

        sl = slice(h * dh, (h + 1) * dh)
        q = q_ref[:, sl]
        k = k_ref[:, sl]
        v = v_ref[:, sl]
        s = lax.dot_general(q, k, (((1,), (1,)), ((), ())), preferred_element_type=F32)
        sw = s * w_intra
        c_prev = c_scr[h]
        n_prev = n_scr[h]
        num = (jnp.dot(sw.astype(BF16), v, preferred_element_type=F32)
               + w_inter * jnp.dot(q, c_prev.astype(BF16), preferred_element_type=F32))
        qn = lax.dot_general(q, n_prev.astype(BF16), (((1,), (1,)), ((), ())),
                             preferred_element_type=F32)
        den = jnp.sum(sw, axis=1, keepdims=True) + w_inter * qn
        den = jnp.maximum(jnp.abs(den), jnp.exp2(-m_t))
        mu = jnp.mean(num, axis=1, keepdims=True)
        d = num - mu
        var = jnp.mean(d * d, axis=1, keepdims=True)
        scale = lax.rsqrt(var + GN_EPS * (den * den))
        h_ref[:, sl] = (d * jnp.concatenate([scale] * (dh // LANES), axis=1)).astype(h_ref.dtype)

        g_row = b_last + r_row
        m_new = jnp.maximum(b_last + m_prev, jnp.max(g_row, axis=1, keepdims=True))
        decay = jnp.exp2(b_last + m_prev - m_new)
        wg_row = jnp.exp2(g_row - m_new).astype(BF16)
        kt_w = jnp.transpose(k) * wg_row
        c_scr[h] = decay * c_prev + jnp.dot(kt_w, v, preferred_element_type=F32)
        n_scr[h] = decay * n_prev + jnp.dot(jnp.broadcast_to(wg_row, (LANES, chunk)), k,
                                            preferred_element_type=F32)
        m_scr[h] = jnp.broadcast_to(m_new, m_scr.shape[1:])


def _mlstm(q, k, v, gates, *, batch, seq, heads, chunk):
    t, inner = q.shape
    dh = inner // heads
    chunk = min(chunk, seq)
    assert seq % chunk == 0
    nc = seq // chunk
    g_rows = gates[:, :2 * heads].reshape(batch * nc, chunk, 2 * heads).transpose(0, 2, 1)
    blk = pl.BlockSpec((chunk, inner), lambda b, c: (b * nc + c, 0))
    kernel = functools.partial(_mlstm_kernel, chunk=chunk, heads=heads, dh=dh)
    return pl.pallas_call(
        kernel,
        grid=(batch, nc),
        in_specs=[blk, blk, blk,
                  pl.BlockSpec((1, 2 * heads, chunk), lambda b, c: (b * nc + c, 0, 0))],
        out_specs=blk,
        out_shape=jax.ShapeDtypeStruct((t, inner), BF16),
        scratch_shapes=[pltpu.VMEM((heads, dh, dh), F32),
                        pltpu.VMEM((heads, LANES, dh), F32),
                        pltpu.VMEM((heads, 8, LANES), F32)],
        compiler_params=_params("arbitrary", "arbitrary"),
        name="mlstm_scan",
    )(q, k, v, g_rows)


def _mlstm_out_kernel(hn_ref, o_ref, z_ref, xc_ref, x_ref, gnw_ref, skip_ref, w_ref, g_ref, b_ref,
                      y_ref, *, alpha, sub):
    gnw_half = (0.5 * gnw_ref[...]).astype(BF16)
    skip = skip_ref[...].astype(BF16)
    for r0 in range(0, hn_ref.shape[0], sub):
        rows = slice(r0, r0 + sub)
        a = hn_ref[rows, :] * gnw_half
        hh = a + a * jnp.tanh(o_ref[rows, :] * 0.5) + skip * xc_ref[rows, :]
        hz = hh * _silu(z_ref[rows, :])
        y = jnp.dot(hz, w_ref[...], preferred_element_type=F32)
        y_ref[rows, :] = _layer_norm(alpha * x_ref[rows, :] + y, g_ref[...], b_ref[...])


def _mlstm_out(hn, u, xc, x, gn_w, skip, w_out, ln_g, ln_b, *, alpha, tm):
    t, inner = hn.shape
    d = x.shape[1]
    tm = min(tm, t)
    row = lambda cb: pl.BlockSpec((tm, inner), lambda i, cb=cb: (i, cb))
    vec = lambda n: pl.BlockSpec((1, n), lambda i: (0, 0))
    return pl.pallas_call(
        functools.partial(_mlstm_out_kernel, alpha=alpha, sub=min(OUT_SUB_ROWS, tm)),
        grid=(t // tm,),
        in_specs=[row(0), row(2), row(1), row(0),
                  pl.BlockSpec((tm, d), lambda i: (i, 0)),
                  vec(inner), vec(inner),
                  pl.BlockSpec((inner, d), lambda i: (0, 0)),
                  vec(d), vec(d)],
        out_specs=pl.BlockSpec((tm, d), lambda i: (i, 0)),
        out_shape=jax.ShapeDtypeStruct((t, d), F32),
        compiler_params=_params("parallel"),
        name="mlstm_out_ln",
    )(hn, u, u, xc, x, gn_w.reshape(1, inner), skip.reshape(1, inner), w_out,
      ln_g.reshape(1, d), ln_b.reshape(1, d))


def _mlstm_layer(x, p, *, batch, seq, alpha):
    heads, dh, _ = p["w_q"].shape
    inner = heads * dh
    u = _matmul(x, p["w_in"][:, :3 * inner], bm=1024, bn=1024, out_dtype=BF16, name="mlstm_in_proj")
    gates = _matmul(x, p["w_gate"], bm=1024, bn=GATE_PAD, out_dtype=F32, name="mlstm_gate_proj",
                    bias=p["b_gate"])
    xc, q, k, v = _conv_qkv(u, p["conv_w"], p["conv_b"], p["w_q"], p["w_k"], p["w_v"],
                            seq=seq, tm=1024)
    hn = _mlstm(q, k, v, gates, batch=batch, seq=seq, heads=heads, chunk=MLSTM_CHUNK)
    return _mlstm_out(hn, u, xc, x, p["gn_w"], p["skip"], p["w_out"], p["ln_g"], p["ln_b"],
                      alpha=alpha, tm=512)


def _bias_table_kernel(r_ref, o_ref, *, qblk, width, past):
    rolled = pltpu.roll(jnp.broadcast_to(r_ref[0], (qblk, r_ref.shape[2])), 0, 1,
                        stride=1, stride_axis=0)
    bias = rolled[:, :width]
    cq = lax.broadcasted_iota(jnp.int32, (qblk, width), 0) // ATTN_CHUNK
    col = lax.broadcasted_iota(jnp.int32, (qblk, width), 1)
    ck = col // ATTN_CHUNK
    band = (ck >= cq) & (ck <= cq + ATTN_PAST_CHUNKS)
    for variant in range(o_ref.shape[0]):
        first_valid = 0 if variant == 0 else past - (variant - 1) * qblk
        o_ref[variant, 0] = jnp.where(band & (col >= first_valid), bias, -jnp.inf)


def _bias_tables(rel_bias, *, qblk, past):
    heads, nrel = rel_bias.shape
    clip = (nrel - 1) // 2
    width = past + qblk
    rlen = 1 << (width + qblk - 1).bit_length()
    assert past >= clip and width - past >= 0
    far_past = jnp.broadcast_to(rel_bias[:, nrel - 1:], (heads, past - clip))
    future = jnp.broadcast_to(rel_bias[:, :1], (heads, width - (past + clip)))
    wrapped = jnp.broadcast_to(rel_bias[:, nrel - 1:], (heads, rlen - width - 1))
    r = jnp.concatenate([far_past, rel_bias[:, ::-1], future, wrapped], axis=1)
    assert r.shape[1] == rlen
    n_var = past // qblk + 1
    return pl.pallas_call(
        functools.partial(_bias_table_kernel, qblk=qblk, width=width, past=past),
        grid=(heads,),
        in_specs=[pl.BlockSpec((1, 1, rlen), lambda h: (h, 0, 0))],
        out_specs=pl.BlockSpec((n_var, 1, qblk, width), lambda h: (0, h, 0, 0)),
        out_shape=jax.ShapeDtypeStruct((n_var, heads, qblk, width), F32),
        compiler_params=_params("parallel"),
        name="attn_bias_table",
    )(r.reshape(heads, 1, rlen).astype(F32))


def _attn_kernel(q_ref, kp_ref, kc_ref, vp_ref, vc_ref, bias_ref, o_ref,
                 *, qblk, past, steps_per_seq, dh):
    first = (pl.program_id(1) % steps_per_seq) == 0
    rows = q_ref.shape[0]
    n_past = past // qblk
    n_kblk = n_past + 1
    lane = lax.broadcasted_iota(jnp.int32, (qblk, LANES), 1)
    low_half = lane < dh

    def kv_block(prev_ref, cur_ref, j):
        if j < n_past:
            return prev_ref[j * qblk:(j + 1) * qblk, :]
        return cur_ref[(j - n_past) * qblk:(j - n_past + 1) * qblk, :]

    for t in range(rows // qblk):
        q2 = q_ref[t * qblk:(t + 1) * qblk, :]
        q2 = (q2.astype(F32) * (float(dh) ** -0.5)).astype(BF16)
        kb = [kv_block(kp_ref, kc_ref, t + j) for j in range(n_kblk)]
        vb = [kv_block(vp_ref, vc_ref, t + j) for j in range(n_kblk)]
        variant = jnp.where(first, t + 1, 0) if t < n_past else 0
        outs = []
        for e in range(2):
            qe = jnp.where(low_half if e == 0 else jnp.logical_not(low_half), q2, jnp.zeros_like(q2))
            s = jnp.concatenate(
                [lax.dot_general(qe, kb[j], (((1,), (1,)), ((), ())), preferred_element_type=F32)
                 for j in range(n_kblk)], axis=1)
            s = s + bias_ref[variant, e]
            m = jnp.max(s, axis=1, keepdims=True)
            p = jnp.exp(s - m)
            l = jnp.sum(p, axis=1, keepdims=True)
            pb = p.astype(BF16)
            o = jnp.dot(pb[:, 0:qblk], vb[0], preferred_element_type=F32)
            for j in range(1, n_kblk):
                o = o + jnp.dot(pb[:, j * qblk:(j + 1) * qblk], vb[j], preferred_element_type=F32)
            outs.append(o / l)
        o_ref[t * qblk:(t + 1) * qblk, :] = jnp.where(low_half, outs[0], outs[1]).astype(o_ref.dtype)


def _attention(qg, kv, tables, *, seq, heads, dh):
    t = qg.shape[0]
    inner = heads * dh
    assert 2 * dh == LANES and heads % 2 == 0
    pairs = heads // 2
    qblk = ATTN_QBLK
    past = ATTN_PAST_CHUNKS * ATTN_CHUNK
    rows = min(ATTN_STEP_ROWS, seq)
    assert seq % rows == 0 and rows % past == 0 and past % qblk == 0 and qblk % ATTN_CHUNK == 0
    steps_per_seq = seq // rows
    ratio = rows // past
    v_off = inner // LANES
    n_var = tables.shape[0]
    width = tables.shape[3]
    prev_idx = lambda i: jnp.maximum(i * ratio - 1, 0)
    kernel = functools.partial(_attn_kernel, qblk=qblk, past=past, steps_per_seq=steps_per_seq, dh=dh)
    return pl.pallas_call(
        kernel,
        grid=(pairs, t // rows),
        in_specs=[pl.BlockSpec((rows, LANES), lambda p, i: (i, p)),
                  pl.BlockSpec((past, LANES), lambda p, i: (prev_idx(i), p)),
                  pl.BlockSpec((rows, LANES), lambda p, i: (i, p)),
                  pl.BlockSpec((past, LANES), lambda p, i: (prev_idx(i), v_off + p)),
                  pl.BlockSpec((rows, LANES), lambda p, i: (i, v_off + p)),
                  pl.BlockSpec((n_var, 2, qblk, width), lambda p, i: (0, p, 0, 0))],
        out_specs=pl.BlockSpec((rows, LANES), lambda p, i: (i, p)),
        out_shape=jax.ShapeDtypeStruct((t, inner), BF16),
        compiler_params=_params("parallel", "arbitrary"),
        name="chunk_attention",
    )(qg, kv, kv, kv, kv, tables)


def _attn_out_kernel(o_ref, g_ref, x_ref, w_ref, lg_ref, lb_ref, y_ref, *, alpha, sub):
    for r0 in range(0, o_ref.shape[0], sub):
        rows = slice(r0, r0 + sub)
        og = o_ref[rows, :] * _silu(g_ref[rows, :])
        y = jnp.dot(og, w_ref[...], preferred_element_type=F32)
        y_ref[rows, :] = _layer_norm(alpha * x_ref[rows, :] + y, lg_ref[...], lb_ref[...])


def _attn_out(o, qg, x, w_out, ln_g, ln_b, *, alpha, tm):
    t, inner = o.shape
    d = x.shape[1]
    tm = min(tm, t)
    vec = pl.BlockSpec((1, d), lambda i: (0, 0))
    return pl.pallas_call(
        functools.partial(_attn_out_kernel, alpha=alpha, sub=min(OUT_SUB_ROWS, tm)),
        grid=(t // tm,),
        in_specs=[pl.BlockSpec((tm, inner), lambda i: (i, 0)),
                  pl.BlockSpec((tm, inner), lambda i: (i, 1)),
                  pl.BlockSpec((tm, d), lambda i: (i, 0)),
                  pl.BlockSpec((inner, d), lambda i: (0, 0)),
                  vec, vec],
        out_specs=pl.BlockSpec((tm, d), lambda i: (i, 0)),
        out_shape=jax.ShapeDtypeStruct((t, d), F32),
        compiler_params=_params("parallel"),
        name="attn_out_ln",
    )(o, qg, x, w_out, ln_g.reshape(1, d), ln_b.reshape(1, d))


def _attn_layer(x, kv, p, *, seq, alpha):
    heads = p["rel_bias"].shape[0]
    inner = p["w_out"].shape[0]
    dh = inner // heads
    qg = _matmul(x, p["w_in"], bm=1024, bn=1024, out_dtype=BF16, name="attn_in_proj")
    tables = _bias_tables(p["rel_bias"], qblk=ATTN_QBLK, past=ATTN_PAST_CHUNKS * ATTN_CHUNK)
    o = _attention(qg, kv, tables, seq=seq, heads=heads, dh=dh)
    return _attn_out(o, qg, x, p["w_out"], p["ln_g"], p["ln_b"], alpha=alpha, tm=1024)


def kernel(x, a_w_in, a_b_gate, a_conv_w, a_conv_b, a_w_q, a_w_k, a_w_v, a_gn_w, a_skip, a_w_out,
           a_ln_g, a_ln_b, kv_w, b_w_in, b_rel_bias, b_w_out, b_ln_g, b_ln_b):
    batch, seq, d = x.shape
    n_a, n_b = a_w_in.shape[0], b_w_in.shape[0]
    alpha = (2.0 * (n_a + n_b)) ** 0.25
    heads = a_w_q.shape[1]
    inner = a_conv_w.shape[2]
    n_gate = 2 * heads
    h = x.reshape(batch * seq, d)
    for l in range(n_a):
        w_gate = jnp.pad(a_w_in[l][:, 3 * inner:], ((0, 0), (0, GATE_PAD - n_gate)))
        p = dict(w_in=a_w_in[l].astype(BF16), w_gate=w_gate.astype(BF16),
                 b_gate=jnp.pad(a_b_gate[l], (0, GATE_PAD - n_gate)),
                 conv_w=a_conv_w[l], conv_b=a_conv_b[l],
                 w_q=a_w_q[l].astype(BF16), w_k=a_w_k[l].astype(BF16), w_v=a_w_v[l].astype(BF16),
                 gn_w=a_gn_w[l], skip=a_skip[l], w_out=a_w_out[l].astype(BF16),
                 ln_g=a_ln_g[l], ln_b=a_ln_b[l])
        h = _mlstm_layer(h, p, batch=batch, seq=seq, alpha=alpha)
    kv = _matmul(h, kv_w.astype(BF16), bm=1024, bn=1024, out_dtype=BF16, name="shared_kv_proj")
    for l in range(n_b):
        p = dict(w_in=b_w_in[l].astype(BF16), rel_bias=b_rel_bias[l], w_out=b_w_out[l].astype(BF16),
                 ln_g=b_ln_g[l], ln_b=b_ln_b[l])
        h = _attn_layer(h, kv, p, seq=seq, alpha=alpha)
    return h.reshape(batch, seq, d)
```

```python
import functools

import jax
import jax.numpy as jnp
from jax import lax
from jax.experimental import pallas as pl
from jax.experimental.pallas import tpu as pltpu

F32 = jnp.float32
BF16 = jnp.bfloat16

LOG2E = 1.4426950408889634
LN_EPS = 1e-5
GN_EPS = 1e-6
ATTN_CHUNK = 64
ATTN_PAST_CHUNKS = 8
LANES = 128
GATE_PAD = 128
VMEM_LIMIT = 48 * 1024 * 1024

CONV_SUB_ROWS = 128
OUT_SUB_ROWS = 512
MLSTM_CHUNK = 256
ATTN_QBLK = 256
ATTN_STEP_ROWS = 2048


def _params(*sem):
    return pltpu.CompilerParams(dimension_semantics=sem, vmem_limit_bytes=VMEM_LIMIT)


def _sigmoid(x):
    return 0.5 * jnp.tanh(0.5 * x) + 0.5


def _silu(x):
    h = 0.5 * x
    return h * jnp.tanh(h) + h


def _log_sigmoid(x):
    return jnp.minimum(x, 0.0) - jnp.log1p(jnp.exp(-jnp.abs(x)))


def _split3(x):
    hi = x.astype(BF16)
    r1 = x - hi.astype(F32)
    mid = r1.astype(BF16)
    lo = (r1 - mid.astype(F32)).astype(BF16)
    return hi, mid, lo


def _layer_norm(r, g, b):
    mu = jnp.mean(r, axis=-1, keepdims=True)
    d = r - mu
    var = jnp.mean(d * d, axis=-1, keepdims=True)
    return d * lax.rsqrt(var + LN_EPS) * g + b


def _mm_kernel(x_ref, w_ref, o_ref):
    acc = jnp.dot(x_ref[...].astype(BF16), w_ref[...], preferred_element_type=F32)
    o_ref[...] = acc.astype(o_ref.dtype)


def _mm_gate_kernel(x_ref, w_ref, wg_ref, bg_ref, o_ref, g_ref):
    x = x_ref[...].astype(BF16)
    o_ref[...] = jnp.dot(x, w_ref[...], preferred_element_type=F32).astype(o_ref.dtype)

    @pl.when(pl.program_id(1) == 0)
    def _():
        g_ref[...] = jnp.dot(x, wg_ref[...], preferred_element_type=F32) + bg_ref[...]


def _col_block(n, bn):
    bn = min(bn, n)
    while n % bn:
        bn -= LANES
    assert bn > 0
    return bn


def _matmul(x, w, *, bm, bn, out_dtype, name):
    m, k = x.shape
    n = w.shape[1]
    bm, bn = min(bm, m), _col_block(n, bn)
    assert m % bm == 0
    return pl.pallas_call(
        _mm_kernel,
        grid=(m // bm, n // bn),
        in_specs=[pl.BlockSpec((bm, k), lambda i, j: (i, 0)),
                  pl.BlockSpec((k, bn), lambda i, j: (0, j))],
        out_specs=pl.BlockSpec((bm, bn), lambda i, j: (i, j)),
        out_shape=jax.ShapeDtypeStruct((m, n), out_dtype),
        compiler_params=_params("parallel", "arbitrary"),
        name=name,
    )(x, w)


def _in_proj_with_gates(x, w, w_gate, b_gate, *, n, bm, bn, name):
    m, k = x.shape
    bm, bn = min(bm, m), _col_block(n, bn)
    ng = w_gate.shape[1]
    assert m % bm == 0
    return pl.pallas_call(
        _mm_gate_kernel,
        grid=(m // bm, n // bn),
        in_specs=[pl.BlockSpec((bm, k), lambda i, j: (i, 0)),
                  pl.BlockSpec((k, bn), lambda i, j: (0, j)),
                  pl.BlockSpec((k, ng), lambda i, j: (0, 0)),
                  pl.BlockSpec((1, ng), lambda i, j: (0, 0))],
        out_specs=[pl.BlockSpec((bm, bn), lambda i, j: (i, j)),
                   pl.BlockSpec((bm, ng), lambda i, j: (i, 0))],
        out_shape=[jax.ShapeDtypeStruct((m, n), BF16), jax.ShapeDtypeStruct((m, ng), F32)],
        compiler_params=_params("parallel", "arbitrary"),
        name=name,
    )(x, w, w_gate, b_gate.reshape(1, ng))


def _conv_qkv_kernel(xm_ref, cw_ref, cb_ref, wq_ref, wk_ref, wv_ref,
                     xc_ref, q_ref, k_ref, v_ref, xbuf, *, tm, sub, tiles_per_seq, taps, kscale):
    i = pl.program_id(1)
    halo = xbuf.shape[0]

    @pl.when(i % tiles_per_seq == 0)
    def _():
        xbuf[...] = jnp.zeros(xbuf.shape, F32)

    prev = xbuf[...]
    for r0 in range(0, tm, sub):
        rows = slice(r0, r0 + sub)
        xm = xm_ref[rows, :]
        x = xm.astype(F32)
        xe = jnp.concatenate([prev, x], axis=0)
        prev = x[sub - halo:sub, :]
        acc = cb_ref[...] + x * cw_ref[taps - 1:taps, :]
        for shift in range(1, taps):
            xs = pltpu.roll(xe, shift, 0)[halo:halo + sub, :]
            acc = acc + xs * cw_ref[taps - 1 - shift:taps - shift, :]
        xc = _silu(acc).astype(BF16)
        xc_ref[rows, :] = xc
        q_ref[rows, :] = jnp.dot(xc, wq_ref[...], preferred_element_type=F32).astype(BF16)
        k_ref[rows, :] = (jnp.dot(xc, wk_ref[...], preferred_element_type=F32) * kscale).astype(BF16)
        v_ref[rows, :] = jnp.dot(xm, wv_ref[...], preferred_element_type=F32).astype(BF16)
    xbuf[...] = prev


def _conv_qkv(u, conv_w, conv_b, w_q, w_k, w_v, *, seq, tm):
    t = u.shape[0]
    heads, dh, _ = w_q.shape
    inner = heads * dh
    taps = conv_w.shape[0]
    tm = min(tm, seq)
    assert seq % tm == 0 and taps - 1 <= 8
    out = jax.ShapeDtypeStruct((t, inner), BF16)
    row_blk = pl.BlockSpec((tm, dh), lambda h, i: (i, h))
    w_blk = pl.BlockSpec((None, dh, dh), lambda h, i: (h, 0, 0))
    sub = min(CONV_SUB_ROWS, tm)
    assert tm % sub == 0
    kernel = functools.partial(_conv_qkv_kernel, tm=tm, sub=sub, tiles_per_seq=seq // tm, taps=taps,
                               kscale=float(dh) ** -0.5)
    return pl.pallas_call(
        kernel,
        grid=(heads, t // tm),
        in_specs=[row_blk,
                  pl.BlockSpec((taps, dh), lambda h, i: (0, h)),
                  pl.BlockSpec((1, dh), lambda h, i: (0, h)),
                  w_blk, w_blk, w_blk],
        out_specs=[row_blk, row_blk, row_blk, row_blk],
        out_shape=[out, out, out, out],
        scratch_shapes=[pltpu.VMEM((8, dh), F32)],
        compiler_params=_params("arbitrary", "arbitrary"),
        name="mlstm_conv_qkv",
    )(u, conv_w, conv_b.reshape(1, inner), w_q, w_k, w_v)


def _mlstm_kernel(q_ref, k_ref, v_ref, g_ref, h_ref, c_scr, n_scr, m_scr, *, chunk, heads, dh):
    @pl.when(pl.program_id(1) == 0)
    def _():
        c_scr[...] = jnp.zeros(c_scr.shape, F32)
        n_scr[...] = jnp.zeros(n_scr.shape, F32)
        m_scr[...] = jnp.zeros(m_scr.shape, F32)

    t_id = lax.broadcasted_iota(jnp.int32, (chunk, chunk), 0)
    s_id = lax.broadcasted_iota(jnp.int32, (chunk, chunk), 1)
    causal = s_id <= t_id
    triu = jnp.where(t_id <= s_id, 1.0, 0.0).astype(BF16)

    grow = g_ref[0]
    i_rows = grow[:heads, :] * LOG2E
    lf_rows = _log_sigmoid(grow[heads:, :]) * LOG2E
    br = jnp.dot(jnp.concatenate(_split3(lf_rows), axis=0), triu, preferred_element_type=F32)
    bcum_rows = br[:heads] + br[heads:2 * heads] + br[2 * heads:]

    for h in range(heads):
        lf_row = lf_rows[h:h + 1, :]
        bcum_col = jnp.sum(jnp.where(causal, lf_row, 0.0), axis=1, keepdims=True)
        m_prev = m_scr[h, 0:1, 0:1]
        b_last = bcum_col[chunk - 1:chunk, :]
        r_row = i_rows[h:h + 1, :] - bcum_rows[h:h + 1, :]
        dmat = jnp.where(causal, bcum_col + r_row, -jnp.inf)
        m_inter = bcum_col + m_prev
        m_t = jnp.maximum(m_inter, jnp.max(dmat, axis=1, keepdims=True))
        w_intra = jnp.exp2(dmat - m_t)
        w_inter = jnp.exp2(m_inter - m_t)

        sl = slice(h * dh, (h + 1) * dh)
        q = q_ref[:, sl]
        k = k_ref[:, sl]
        v = v_ref[:, sl]
        s = lax.dot_general(q, k, (((1,), (1,)), ((), ())), preferred_element_type=F32)
        sw = s * w_intra
        c_prev = c_scr[h]
        n_prev = n_scr[h]
        num = (jnp.dot(sw.astype(BF16), v, preferred_element_type=F32)
               + w_inter * jnp.dot(q, c_prev.astype(BF16), preferred_element_type=F32))
        qn = lax.dot_general(q, n_prev.astype(BF16), (((1,), (1,)), ((), ())),
                             preferred_element_type=F32)
        den = jnp.sum(sw, axis=1, keepdims=True) + w_inter * qn
        den = jnp.maximum(jnp.abs(den), jnp.exp2(-m_t))
        mu = jnp.mean(num, axis=1, keepdims=True)
        d = num - mu
        var = jnp.mean(d * d, axis=1, keepdims=True)
        scale = lax.rsqrt(var + GN_EPS * (den * den))
        h_ref[:, sl] = (d * jnp.concatenate([scale] * (dh // LANES), axis=1)).astype(h_ref.dtype)

        g_row = b_last + r_row
        m_new = jnp.maximum(b_last + m_prev, jnp.max(g_row, axis=1, keepdims=True))
        decay = jnp.exp2(b_last + m_prev - m_new)
        wg_row = jnp.exp2(g_row - m_new).astype(BF16)
        kt_w = jnp.transpose(k) * wg_row
        c_scr[h] = decay * c_prev + jnp.dot(kt_w, v, preferred_element_type=F32)
        n_scr[h] = decay * n_prev + jnp.dot(jnp.broadcast_to(wg_row, (LANES, chunk)), k,
                                            preferred_element_type=F32)
        m_scr[h] = jnp.broadcast_to(m_new, m_scr.shape[1:])


def _mlstm(q, k, v, gates, *, batch, seq, heads, chunk):
    t, inner = q.shape
    dh = inner // heads
    chunk = min(chunk, seq)
    assert seq % chunk == 0
    nc = seq // chunk
    g_rows = gates[:, :2 * heads].reshape(batch * nc, chunk, 2 * heads).transpose(0, 2, 1)
    blk = pl.BlockSpec((chunk, inner), lambda b, c: (b * nc + c, 0))
    kernel = functools.partial(_mlstm_kernel, chunk=chunk, heads=heads, dh=dh)
    return pl.pallas_call(
        kernel,
        grid=(batch, nc),
        in_specs=[blk, blk, blk,
                  pl.BlockSpec((1, 2 * heads, chunk), lambda b, c: (b * nc + c, 0, 0))],
        out_specs=blk,
        out_shape=jax.ShapeDtypeStruct((t, inner), BF16),
        scratch_shapes=[pltpu.VMEM((heads, dh, dh), F32),
                        pltpu.VMEM((heads, LANES, dh), F32),
                        pltpu.VMEM((heads, 8, LANES), F32)],
        compiler_params=_params("arbitrary", "arbitrary"),
        name="mlstm_scan",
    )(q, k, v, g_rows)


def _mlstm_out_kernel(hn_ref, o_ref, z_ref, xc_ref, x_ref, gnw_ref, skip_ref, w_ref, g_ref, b_ref,
                      y_ref, *, alpha, sub):
    gnw_half = (0.5 * gnw_ref[...]).astype(BF16)
    skip = skip_ref[...].astype(BF16)
    for r0 in range(0, hn_ref.shape[0], sub):
        rows = slice(r0, r0 + sub)
        a = hn_ref[rows, :] * gnw_half
        hh = a + a * jnp.tanh(o_ref[rows, :] * 0.5) + skip * xc_ref[rows, :]
        hz = hh * _silu(z_ref[rows, :])
        y = jnp.dot(hz, w_ref[...], preferred_element_type=F32)
        y_ref[rows, :] = _layer_norm(alpha * x_ref[rows, :] + y, g_ref[...], b_ref[...])


def _mlstm_out(hn, u, xc, x, gn_w, skip, w_out, ln_g, ln_b, *, alpha, tm):
    t, inner = hn.shape
    d = x.shape[1]
    tm = min(tm, t)
    row = lambda cb: pl.BlockSpec((tm, inner), lambda i, cb=cb: (i, cb))
    vec = lambda n: pl.BlockSpec((1, n), lambda i: (0, 0))
    return pl.pallas_call(
        functools.partial(_mlstm_out_kernel, alpha=alpha, sub=min(OUT_SUB_ROWS, tm)),
        grid=(t // tm,),
        in_specs=[row(0), row(2), row(1), row(0),
                  pl.BlockSpec((tm, d), lambda i: (i, 0)),
                  vec(inner), vec(inner),
                  pl.BlockSpec((inner, d), lambda i: (0, 0)),
                  vec(d), vec(d)],
        out_specs=pl.BlockSpec((tm, d), lambda i: (i, 0)),
        out_shape=jax.ShapeDtypeStruct((t, d), F32),
        compiler_params=_params("parallel"),
        name="mlstm_out_ln",
    )(hn, u, u, xc, x, gn_w.reshape(1, inner), skip.reshape(1, inner), w_out,
      ln_g.reshape(1, d), ln_b.reshape(1, d))


def _mlstm_layer(x, p, *, batch, seq, alpha):
    heads, dh, _ = p["w_q"].shape
    inner = heads * dh
    u, gates = _in_proj_with_gates(x, p["w_in"], p["w_gate"], p["b_gate"], n=3 * inner,
                                   bm=1024, bn=1024, name="mlstm_in_proj")
    xc, q, k, v = _conv_qkv(u, p["conv_w"], p["conv_b"], p["w_q"], p["w_k"], p["w_v"],
                            seq=seq, tm=1024)
    hn = _mlstm(q, k, v, gates, batch=batch, seq=seq, heads=heads, chunk=MLSTM_CHUNK)
    return _mlstm_out(hn, u, xc, x, p["gn_w"], p["skip"], p["w_out"], p["ln_g"], p["ln_b"],
                      alpha=alpha, tm=512)


def _bias_table_kernel(r_ref, o_ref, *, qblk, width, past):
    rolled = pltpu.roll(jnp.broadcast_to(r_ref[0] * LOG2E, (qblk, r_ref.shape[2])), 0, 1,
                        stride=1, stride_axis=0)
    bias = rolled[:, :width]
    cq = lax.broadcasted_iota(jnp.int32, (qblk, width), 0) // ATTN_CHUNK
    col = lax.broadcasted_iota(jnp.int32, (qblk, width), 1)
    ck = col // ATTN_CHUNK
    band = (ck >= cq) & (ck <= cq + ATTN_PAST_CHUNKS)
    for variant in range(o_ref.shape[0]):
        first_valid = 0 if variant == 0 else past - (variant - 1) * qblk
        o_ref[variant, 0] = jnp.where(band & (col >= first_valid), bias, -jnp.inf)


def _bias_tables(rel_bias, *, qblk, past):
    heads, nrel = rel_bias.shape
    clip = (nrel - 1) // 2
    width = past + qblk
    rlen = 1 << (width + qblk - 1).bit_length()
    assert past >= clip and width - past >= 0
    far_past = jnp.broadcast_to(rel_bias[:, nrel - 1:], (heads, past - clip))
    future = jnp.broadcast_to(rel_bias[:, :1], (heads, width - (past + clip)))
    wrapped = jnp.broadcast_to(rel_bias[:, nrel - 1:], (heads, rlen - width - 1))
    r = jnp.concatenate([far_past, rel_bias[:, ::-1], future, wrapped], axis=1)
    assert r.shape[1] == rlen
    n_var = past // qblk + 1
    return pl.pallas_call(
        functools.partial(_bias_table_kernel, qblk=qblk, width=width, past=past),
        grid=(heads,),
        in_specs=[pl.BlockSpec((1, 1, rlen), lambda h: (h, 0, 0))],
        out_specs=pl.BlockSpec((n_var, 1, qblk, width), lambda h: (0, h, 0, 0)),
        out_shape=jax.ShapeDtypeStruct((n_var, heads, qblk, width), F32),
        compiler_params=_params("parallel"),
        name="attn_bias_table",
    )(r.reshape(heads, 1, rlen).astype(F32))


def _attn_kernel(q_ref, kp_ref, kc_ref, vp_ref, vc_ref, bias_ref, o_ref,
                 *, qblk, past, steps_per_seq, dh):
    first = (pl.program_id(1) % steps_per_seq) == 0
    rows = q_ref.shape[0]
    n_past = past // qblk
    n_kblk = n_past + 1
    lane = lax.broadcasted_iota(jnp.int32, (qblk, LANES), 1)
    low_half = lane < dh

    def kv_block(prev_ref, cur_ref, j):
        if j < n_past:
            return prev_ref[j * qblk:(j + 1) * qblk, :]
        return cur_ref[(j - n_past) * qblk:(j - n_past + 1) * qblk, :]

    n_sub = rows // qblk
    qscale = LOG2E * float(dh) ** -0.5
    ones = jnp.ones((qblk, LANES), BF16)
    v_blocks = [kv_block(vp_ref, vc_ref, j) for j in range(n_sub + n_past)]
    v_heads = [[jnp.where(low_half, vb, ones) for vb in v_blocks],
               [jnp.where(low_half, ones, vb) for vb in v_blocks]]

    def scores(t):
        q2 = (q_ref[t * qblk:(t + 1) * qblk, :].astype(F32) * qscale).astype(BF16)
        zero = jnp.zeros_like(q2)
        qq = jnp.concatenate([jnp.where(low_half, q2, zero), jnp.where(low_half, zero, q2)], axis=0)
        k_win = jnp.concatenate([kv_block(kp_ref, kc_ref, t + j) for j in range(n_kblk)], axis=0)
        return lax.dot_general(qq, k_win, (((1,), (1,)), ((), ())), preferred_element_type=F32)

    def finish(t, s2):
        variant = jnp.where(first, t + 1, 0) if t < n_past else 0
        outs = []
        for e in range(2):
            s = s2[e * qblk:(e + 1) * qblk, :] + bias_ref[variant, e]
            p = jnp.exp2(s - jnp.max(s, axis=1, keepdims=True)).astype(BF16)
            v_win = jnp.concatenate(v_heads[e][t:t + n_kblk], axis=0)
            outs.append(jnp.dot(p, v_win, preferred_element_type=F32))
        o = jnp.where(low_half, outs[0], outs[1])
        l = pltpu.roll(jnp.where(low_half, outs[1], outs[0]), dh, 1)
        o_ref[t * qblk:(t + 1) * qblk, :] = (o / l).astype(o_ref.dtype)

    s_next = scores(0)
    for t in range(n_sub):
        s_cur = s_next
        if t + 1 < n_sub:
            s_next = scores(t + 1)
        finish(t, s_cur)


def _attention(qg, kv, tables, *, seq, heads, dh):
    t = qg.shape[0]
    inner = heads * dh
    assert 2 * dh == LANES and heads % 2 == 0
    pairs = heads // 2
    qblk = ATTN_QBLK
    past = ATTN_PAST_CHUNKS * ATTN_CHUNK
    rows = min(ATTN_STEP_ROWS, seq)
    assert seq % rows == 0 and rows % past == 0 and past % qblk == 0 and qblk % ATTN_CHUNK == 0
    steps_per_seq = seq // rows
    ratio = rows // past
    v_off = inner // LANES
    n_var = tables.shape[0]
    width = tables.shape[3]
    prev_idx = lambda i: jnp.maximum(i * ratio - 1, 0)
    kernel = functools.partial(_attn_kernel, qblk=qblk, past=past, steps_per_seq=steps_per_seq, dh=dh)
    return pl.pallas_call(
        kernel,
        grid=(pairs, t // rows),
        in_specs=[pl.BlockSpec((rows, LANES), lambda p, i: (i, p)),
                  pl.BlockSpec((past, LANES), lambda p, i: (prev_idx(i), p)),
                  pl.BlockSpec((rows, LANES), lambda p, i: (i, p)),
                  pl.BlockSpec((past, LANES), lambda p, i: (prev_idx(i), v_off + p)),
                  pl.BlockSpec((rows, LANES), lambda p, i: (i, v_off + p)),
                  pl.BlockSpec((n_var, 2, qblk, width), lambda p, i: (0, p, 0, 0))],
        out_specs=pl.BlockSpec((rows, LANES), lambda p, i: (i, p)),
        out_shape=jax.ShapeDtypeStruct((t, inner), BF16),
        compiler_params=_params("parallel", "arbitrary"),
        name="chunk_attention",
    )(qg, kv, kv, kv, kv, tables)


def _attn_out_kernel(o_ref, g_ref, x_ref, w_ref, lg_ref, lb_ref, y_ref, *, alpha, sub):
    for r0 in range(0, o_ref.shape[0], sub):
        rows = slice(r0, r0 + sub)
        og = o_ref[rows, :] * _silu(g_ref[rows, :])
        y = jnp.dot(og, w_ref[...], preferred_element_type=F32)
        y_ref[rows, :] = _layer_norm(alpha * x_ref[rows, :] + y, lg_ref[...], lb_ref[...])


def _attn_out(o, qg, x, w_out, ln_g, ln_b, *, alpha, tm):
    t, inner = o.shape
    d = x.shape[1]
    tm = min(tm, t)
    vec = pl.BlockSpec((1, d), lambda i: (0, 0))
    return pl.pallas_call(
        functools.partial(_attn_out_kernel, alpha=alpha, sub=min(OUT_SUB_ROWS, tm)),
        grid=(t // tm,),
        in_specs=[pl.BlockSpec((tm, inner), lambda i: (i, 0)),
                  pl.BlockSpec((tm, inner), lambda i: (i, 1)),
                  pl.BlockSpec((tm, d), lambda i: (i, 0)),
                  pl.BlockSpec((inner, d), lambda i: (0, 0)),
                  vec, vec],
        out_specs=pl.BlockSpec((tm, d), lambda i: (i, 0)),
        out_shape=jax.ShapeDtypeStruct((t, d), F32),
        compiler_params=_params("parallel"),
        name="attn_out_ln",
    )(o, qg, x, w_out, ln_g.reshape(1, d), ln_b.reshape(1, d))


def _attn_layer(x, kv, p, *, seq, alpha):
    heads = p["rel_bias"].shape[0]
    inner = p["w_out"].shape[0]
    dh = inner // heads
    qg = _matmul(x, p["w_in"], bm=1024, bn=1024, out_dtype=BF16, name="attn_in_proj")
    tables = _bias_tables(p["rel_bias"], qblk=ATTN_QBLK, past=ATTN_PAST_CHUNKS * ATTN_CHUNK)
    o = _attention(qg, kv, tables, seq=seq, heads=heads, dh=dh)
    return _attn_out(o, qg, x, p["w_out"], p["ln_g"], p["ln_b"], alpha=alpha, tm=1024)


def kernel(x, a_w_in, a_b_gate, a_conv_w, a_conv_b, a_w_q, a_w_k, a_w_v, a_gn_w, a_skip, a_w_out,
           a_ln_g, a_ln_b, kv_w, b_w_in, b_rel_bias, b_w_out, b_ln_g, b_ln_b):
    batch, seq, d = x.shape
    n_a, n_b = a_w_in.shape[0], b_w_in.shape[0]
    alpha = (2.0 * (n_a + n_b)) ** 0.25
    heads = a_w_q.shape[1]
    inner = a_conv_w.shape[2]
    n_gate = 2 * heads
    h = x.reshape(batch * seq, d)
    for l in range(n_a):
        w_gate = jnp.pad(a_w_in[l][:, 3 * inner:], ((0, 0), (0, GATE_PAD - n_gate)))
        p = dict(w_in=a_w_in[l].astype(BF16), w_gate=w_gate.astype(BF16),
                 b_gate=jnp.pad(a_b_gate[l], (0, GATE_PAD - n_gate)),
                 conv_w=a_conv_w[l], conv_b=a_conv_b[l],
                 w_q=a_w_q[l].astype(BF16), w_k=a_w_k[l].astype(BF16), w_v=a_w_v[l].astype(BF16),
                 gn_w=a_gn_w[l], skip=a_skip[l], w_out=a_w_out[l].astype(BF16),
                 ln_g=a_ln_g[l], ln_b=a_ln_b[l])
        h = _mlstm_layer(h, p, batch=batch, seq=seq, alpha=alpha)
    kv = _matmul(h, kv_w.astype(BF16), bm=1024, bn=1024, out_dtype=BF16, name="shared_kv_proj")
    for l in range(n_b):
        p = dict(w_in=b_w_in[l].astype(BF16), rel_bias=b_rel_bias[l], w_out=b_w_out[l].astype(BF16),
                 ln_g=b_ln_g[l], ln_b=b_ln_b[l])
        h = _attn_layer(h, kv, p, seq=seq, alpha=alpha)
    return h.reshape(batch, seq, d)
```

```python
import functools

import jax
import jax.numpy as jnp
from jax import lax
from jax.experimental import pallas as pl
from jax.experimental.pallas import tpu as pltpu

F32 = jnp.float32
BF16 = jnp.bfloat16

LOG2E = 1.4426950408889634
LN_EPS = 1e-5
GN_EPS = 1e-6
ATTN_CHUNK = 64
ATTN_PAST_CHUNKS = 8
LANES = 128
GATE_PAD = 128
VMEM_LIMIT = 48 * 1024 * 1024

CONV_SUB_ROWS = 128
OUT_SUB_ROWS = 512
MLSTM_CHUNK = 256
ATTN_QBLK = 256
ATTN_STEP_ROWS = 2048


def _params(*sem):
    return pltpu.CompilerParams(dimension_semantics=sem, vmem_limit_bytes=VMEM_LIMIT)


def _sigmoid(x):
    return 0.5 * jnp.tanh(0.5 * x) + 0.5


def _silu(x):
    h = 0.5 * x
    return h * jnp.tanh(h) + h


def _log_sigmoid(x):
    return jnp.minimum(x, 0.0) - jnp.log1p(jnp.exp(-jnp.abs(x)))


def _split3(x):
    hi = x.astype(BF16)
    r1 = x - hi.astype(F32)
    mid = r1.astype(BF16)
    lo = (r1 - mid.astype(F32)).astype(BF16)
    return hi, mid, lo


def _layer_norm(r, g, b):
    mu = jnp.mean(r, axis=-1, keepdims=True)
    d = r - mu
    var = jnp.mean(d * d, axis=-1, keepdims=True)
    return d * lax.rsqrt(var + LN_EPS) * g + b


def _mm_kernel(x_ref, w_ref, o_ref):
    acc = jnp.dot(x_ref[...].astype(BF16), w_ref[...], preferred_element_type=F32)
    o_ref[...] = acc.astype(o_ref.dtype)


def _mm_gate_kernel(x_ref, w_ref, wg_ref, bg_ref, o_ref, g_ref):
    x = x_ref[...].astype(BF16)
    o_ref[...] = jnp.dot(x, w_ref[...], preferred_element_type=F32).astype(o_ref.dtype)

    @pl.when(pl.program_id(1) == 0)
    def _():
        g_ref[...] = jnp.dot(x, wg_ref[...], preferred_element_type=F32) + bg_ref[...]


def _col_block(n, bn):
    bn = min(bn, n)
    while n % bn:
        bn -= LANES
    assert bn > 0
    return bn


def _matmul(x, w, *, bm, bn, out_dtype, name):
    m, k = x.shape
    n = w.shape[1]
    bm, bn = min(bm, m), _col_block(n, bn)
    assert m % bm == 0
    return pl.pallas_call(
        _mm_kernel,
        grid=(m // bm, n // bn),
        in_specs=[pl.BlockSpec((bm, k), lambda i, j: (i, 0)),
                  pl.BlockSpec((k, bn), lambda i, j: (0, j))],
        out_specs=pl.BlockSpec((bm, bn), lambda i, j: (i, j)),
        out_shape=jax.ShapeDtypeStruct((m, n), out_dtype),
        compiler_params=_params("parallel", "arbitrary"),
        name=name,
    )(x, w)


def _in_proj_with_gates(x, w, w_gate, b_gate, *, n, bm, bn, name):
    m, k = x.shape
    bm, bn = min(bm, m), _col_block(n, bn)
    ng = w_gate.shape[1]
    assert m % bm == 0
    return pl.pallas_call(
        _mm_gate_kernel,
        grid=(m // bm, n // bn),
        in_specs=[pl.BlockSpec((bm, k), lambda i, j: (i, 0)),
                  pl.BlockSpec((k, bn), lambda i, j: (0, j)),
                  pl.BlockSpec((k, ng), lambda i, j: (0, 0)),
                  pl.BlockSpec((1, ng), lambda i, j: (0, 0))],
        out_specs=[pl.BlockSpec((bm, bn), lambda i, j: (i, j)),
                   pl.BlockSpec((bm, ng), lambda i, j: (i, 0))],
        out_shape=[jax.ShapeDtypeStruct((m, n), BF16), jax.ShapeDtypeStruct((m, ng), F32)],
        compiler_params=_params("parallel", "arbitrary"),
        name=name,
    )(x, w, w_gate, b_gate.reshape(1, ng))


def _conv_qkv_kernel(xm_ref, cw_ref, cb_ref, wq_ref, wk_ref, wv_ref,
                     xc_ref, q_ref, k_ref, v_ref, xbuf, *, tm, sub, tiles_per_seq, taps):
    i = pl.program_id(1)
    halo = xbuf.shape[0]

    @pl.when(i % tiles_per_seq == 0)
    def _():
        xbuf[...] = jnp.zeros(xbuf.shape, F32)

    def conv(r0, prev):
        xm = xm_ref[r0:r0 + sub, :]
        x = xm.astype(F32)
        xe = jnp.concatenate([prev, x], axis=0)
        acc = cb_ref[...] + x * cw_ref[taps - 1:taps, :]
        for shift in range(1, taps):
            xs = pltpu.roll(xe, shift, 0)[halo:halo + sub, :]
            acc = acc + xs * cw_ref[taps - 1 - shift:taps - shift, :]
        return xm, _silu(acc).astype(BF16), x[sub - halo:sub, :]

    def project(r0, xm, xc):
        rows = slice(r0, r0 + sub)
        xc_ref[rows, :] = xc
        q_ref[rows, :] = jnp.dot(xc, wq_ref[...], preferred_element_type=F32).astype(BF16)
        k_ref[rows, :] = jnp.dot(xc, wk_ref[...], preferred_element_type=F32).astype(BF16)
        v_ref[rows, :] = jnp.dot(xm, wv_ref[...], preferred_element_type=F32).astype(BF16)

    xm, xc, prev = conv(0, xbuf[...])
    for r0 in range(0, tm, sub):
        cur = (xm, xc)
        if r0 + sub < tm:
            xm, xc, prev = conv(r0 + sub, prev)
        project(r0, *cur)
    xbuf[...] = prev


def _conv_qkv(u, conv_w, conv_b, w_q, w_k, w_v, *, seq, tm):
    t = u.shape[0]
    heads, dh, _ = w_q.shape
    inner = heads * dh
    taps = conv_w.shape[0]
    tm = min(tm, seq)
    assert seq % tm == 0 and taps - 1 <= 8
    out = jax.ShapeDtypeStruct((t, inner), BF16)
    row_blk = pl.BlockSpec((tm, dh), lambda h, i: (i, h))
    w_blk = pl.BlockSpec((None, dh, dh), lambda h, i: (h, 0, 0))
    sub = min(CONV_SUB_ROWS, tm)
    assert tm % sub == 0
    kernel = functools.partial(_conv_qkv_kernel, tm=tm, sub=sub, tiles_per_seq=seq // tm, taps=taps)
    return pl.pallas_call(
        kernel,
        grid=(heads, t // tm),
        in_specs=[row_blk,
                  pl.BlockSpec((taps, dh), lambda h, i: (0, h)),
                  pl.BlockSpec((1, dh), lambda h, i: (0, h)),
                  w_blk, w_blk, w_blk],
        out_specs=[row_blk, row_blk, row_blk, row_blk],
        out_shape=[out, out, out, out],
        scratch_shapes=[pltpu.VMEM((8, dh), F32)],
        compiler_params=_params("arbitrary", "arbitrary"),
        name="mlstm_conv_qkv",
    )(u, conv_w, conv_b.reshape(1, inner), w_q, w_k, w_v)


def _mlstm_kernel(q_ref, k_ref, v_ref, g_ref, h_ref, c_scr, n_scr, m_scr, *, chunk, heads, dh):
    @pl.when(pl.program_id(1) == 0)
    def _():
        c_scr[...] = jnp.zeros(c_scr.shape, F32)
        n_scr[...] = jnp.zeros(n_scr.shape, F32)
        m_scr[...] = jnp.zeros(m_scr.shape, F32)

    t_id = lax.broadcasted_iota(jnp.int32, (chunk, chunk), 0)
    s_id = lax.broadcasted_iota(jnp.int32, (chunk, chunk), 1)
    causal = s_id <= t_id
    triu = jnp.where(t_id <= s_id, 1.0, 0.0).astype(BF16)

    grow = g_ref[0]
    i_rows = grow[:heads, :] * LOG2E
    lf_rows = _log_sigmoid(grow[heads:, :]) * LOG2E
    br = jnp.dot(jnp.concatenate(_split3(lf_rows), axis=0), triu, preferred_element_type=F32)
    bcum_rows = br[:heads] + br[heads:2 * heads] + br[2 * heads:]

    def state_products(h):
        sl = slice(h * dh, (h + 1) * dh)
        q = q_ref[:, sl]
        s = lax.dot_general(q, k_ref[:, sl], (((1,), (1,)), ((), ())), preferred_element_type=F32)
        qc = jnp.dot(q, c_scr[h].astype(BF16), preferred_element_type=F32)
        qn = lax.dot_general(q, n_scr[h].astype(BF16), (((1,), (1,)), ((), ())),
                             preferred_element_type=F32)
        return s, qc, qn

    def finish(h, s, qc, qn):
        sl = slice(h * dh, (h + 1) * dh)
        lf_row = lf_rows[h:h + 1, :]
        bcum_col = jnp.sum(jnp.where(causal, lf_row, 0.0), axis=1, keepdims=True)
        m_prev = m_scr[h, 0:1, 0:1]
        b_last = bcum_col[chunk - 1:chunk, :]
        r_row = i_rows[h:h + 1, :] - bcum_rows[h:h + 1, :]
        dmat = jnp.where(causal, bcum_col + r_row, -jnp.inf)
        m_inter = bcum_col + m_prev
        m_t = jnp.maximum(m_inter, jnp.max(dmat, axis=1, keepdims=True))
        w_inter = jnp.exp2(m_inter - m_t)
        sw = s * jnp.exp2(dmat - m_t)
        v = v_ref[:, sl]
        num = jnp.dot(sw.astype(BF16), v, preferred_element_type=F32) + w_inter * qc
        den = jnp.sum(sw, axis=1, keepdims=True) + w_inter * qn
        den = jnp.maximum(jnp.abs(den), jnp.exp2(-m_t))
        mu = jnp.mean(num, axis=1, keepdims=True)
        d = num - mu
        var = jnp.mean(d * d, axis=1, keepdims=True)
        scale = lax.rsqrt(var + GN_EPS * (den * den))
        h_ref[:, sl] = (d * jnp.concatenate([scale] * (dh // LANES), axis=1)).astype(h_ref.dtype)

        g_row = b_last + r_row
        m_new = jnp.maximum(b_last + m_prev, jnp.max(g_row, axis=1, keepdims=True))
        decay = jnp.exp2(b_last + m_prev - m_new)
        wg_row = jnp.exp2(g_row - m_new).astype(BF16)
        k = k_ref[:, sl]
        kt_w = jnp.transpose(k) * wg_row
        c_scr[h] = decay * c_scr[h] + jnp.dot(kt_w, v, preferred_element_type=F32)
        n_scr[h] = decay * n_scr[h] + jnp.dot(jnp.broadcast_to(wg_row, (LANES, chunk)), k,
                                              preferred_element_type=F32)
        m_scr[h] = jnp.broadcast_to(m_new, m_scr.shape[1:])

    nxt = state_products(0)
    for h in range(heads):
        cur = nxt
        if h + 1 < heads:
            nxt = state_products(h + 1)
        finish(h, *cur)


def _mlstm(q, k, v, gates, *, batch, seq, heads, chunk):
    t, inner = q.shape
    dh = inner // heads
    chunk = min(chunk, seq)
    assert seq % chunk == 0
    nc = seq // chunk
    g_rows = gates[:, :2 * heads].reshape(batch * nc, chunk, 2 * heads).transpose(0, 2, 1)
    blk = pl.BlockSpec((chunk, inner), lambda b, c: (b * nc + c, 0))
    kernel = functools.partial(_mlstm_kernel, chunk=chunk, heads=heads, dh=dh)
    return pl.pallas_call(
        kernel,
        grid=(batch, nc),
        in_specs=[blk, blk, blk,
                  pl.BlockSpec((1, 2 * heads, chunk), lambda b, c: (b * nc + c, 0, 0))],
        out_specs=blk,
        out_shape=jax.ShapeDtypeStruct((t, inner), BF16),
        scratch_shapes=[pltpu.VMEM((heads, dh, dh), F32),
                        pltpu.VMEM((heads, LANES, dh), F32),
                        pltpu.VMEM((heads, 8, LANES), F32)],
        compiler_params=_params("arbitrary", "arbitrary"),
        name="mlstm_scan",
    )(q, k, v, g_rows)


def _mlstm_out_kernel(hn_ref, o_ref, z_ref, xc_ref, x_ref, gnw_ref, skip_ref, w_ref, g_ref, b_ref,
                      y_ref, *, alpha, sub):
    gnw_half = (0.5 * gnw_ref[...]).astype(BF16)
    skip = skip_ref[...].astype(BF16)
    for r0 in range(0, hn_ref.shape[0], sub):
        rows = slice(r0, r0 + sub)
        a = hn_ref[rows, :] * gnw_half
        hh = a + a * jnp.tanh(o_ref[rows, :] * 0.5) + skip * xc_ref[rows, :]
        hz = hh * _silu(z_ref[rows, :])
        y = jnp.dot(hz, w_ref[...], preferred_element_type=F32)
        y_ref[rows, :] = _layer_norm(alpha * x_ref[rows, :] + y, g_ref[...], b_ref[...])


def _mlstm_out(hn, u, xc, x, gn_w, skip, w_out, ln_g, ln_b, *, alpha, tm):
    t, inner = hn.shape
    d = x.shape[1]
    tm = min(tm, t)
    row = lambda cb: pl.BlockSpec((tm, inner), lambda i, cb=cb: (i, cb))
    vec = lambda n: pl.BlockSpec((1, n), lambda i: (0, 0))
    return pl.pallas_call(
        functools.partial(_mlstm_out_kernel, alpha=alpha, sub=min(OUT_SUB_ROWS, tm)),
        grid=(t // tm,),
        in_specs=[row(0), row(2), row(1), row(0),
                  pl.BlockSpec((tm, d), lambda i: (i, 0)),
                  vec(inner), vec(inner),
                  pl.BlockSpec((inner, d), lambda i: (0, 0)),
                  vec(d), vec(d)],
        out_specs=pl.BlockSpec((tm, d), lambda i: (i, 0)),
        out_shape=jax.ShapeDtypeStruct((t, d), F32),
        compiler_params=_params("parallel"),
        name="mlstm_out_ln",
    )(hn, u, u, xc, x, gn_w.reshape(1, inner), skip.reshape(1, inner), w_out,
      ln_g.reshape(1, d), ln_b.reshape(1, d))


def _mlstm_layer(x, p, *, batch, seq, alpha):
    heads, dh, _ = p["w_q"].shape
    inner = heads * dh
    u, gates = _in_proj_with_gates(x, p["w_in"], p["w_gate"], p["b_gate"], n=3 * inner,
                                   bm=1024, bn=2048, name="mlstm_in_proj")
    xc, q, k, v = _conv_qkv(u, p["conv_w"], p["conv_b"], p["w_q"], p["w_k"], p["w_v"],
                            seq=seq, tm=2048)
    hn = _mlstm(q, k, v, gates, batch=batch, seq=seq, heads=heads, chunk=MLSTM_CHUNK)
    return _mlstm_out(hn, u, xc, x, p["gn_w"], p["skip"], p["w_out"], p["ln_g"], p["ln_b"],
                      alpha=alpha, tm=512)


def _bias_table_kernel(r_ref, o_ref, *, qblk, width, past):
    rolled = pltpu.roll(jnp.broadcast_to(r_ref[0] * LOG2E, (qblk, r_ref.shape[2])), 0, 1,
                        stride=1, stride_axis=0)
    bias = rolled[:, :width]
    cq = lax.broadcasted_iota(jnp.int32, (qblk, width), 0) // ATTN_CHUNK
    col = lax.broadcasted_iota(jnp.int32, (qblk, width), 1)
    ck = col // ATTN_CHUNK
    band = (ck >= cq) & (ck <= cq + ATTN_PAST_CHUNKS)
    for variant in range(o_ref.shape[0]):
        first_valid = 0 if variant == 0 else past - (variant - 1) * qblk
        o_ref[variant, 0] = jnp.where(band & (col >= first_valid), bias, -jnp.inf)


def _bias_tables(rel_bias, *, qblk, past):
    heads, nrel = rel_bias.shape
    clip = (nrel - 1) // 2
    width = past + qblk
    rlen = 1 << (width + qblk - 1).bit_length()
    assert past >= clip and width - past >= 0
    far_past = jnp.broadcast_to(rel_bias[:, nrel - 1:], (heads, past - clip))
    future = jnp.broadcast_to(rel_bias[:, :1], (heads, width - (past + clip)))
    wrapped = jnp.broadcast_to(rel_bias[:, nrel - 1:], (heads, rlen - width - 1))
    r = jnp.concatenate([far_past, rel_bias[:, ::-1], future, wrapped], axis=1)
    assert r.shape[1] == rlen
    n_var = past // qblk + 1
    return pl.pallas_call(
        functools.partial(_bias_table_kernel, qblk=qblk, width=width, past=past),
        grid=(heads,),
        in_specs=[pl.BlockSpec((1, 1, rlen), lambda h: (h, 0, 0))],
        out_specs=pl.BlockSpec((n_var, 1, qblk, width), lambda h: (0, h, 0, 0)),
        out_shape=jax.ShapeDtypeStruct((n_var, heads, qblk, width), F32),
        compiler_params=_params("parallel"),
        name="attn_bias_table",
    )(r.reshape(heads, 1, rlen).astype(F32))


def _attn_kernel(q_ref, kp_ref, kc_ref, vp_ref, vc_ref, bias_ref, o_ref,
                 *, qblk, past, steps_per_seq, dh):
    first = (pl.program_id(1) % steps_per_seq) == 0
    rows = q_ref.shape[0]
    n_past = past // qblk
    n_kblk = n_past + 1
    lane = lax.broadcasted_iota(jnp.int32, (qblk, LANES), 1)
    low_half = lane < dh

    def kv_block(prev_ref, cur_ref, j):
        if j < n_past:
            return prev_ref[j * qblk:(j + 1) * qblk, :]
        return cur_ref[(j - n_past) * qblk:(j - n_past + 1) * qblk, :]

    n_sub = rows // qblk
    qscale = LOG2E * float(dh) ** -0.5
    ones = jnp.ones((qblk, LANES), BF16)
    v_blocks = [kv_block(vp_ref, vc_ref, j) for j in range(n_sub + n_past)]
    v_heads = [[jnp.where(low_half, vb, ones) for vb in v_blocks],
               [jnp.where(low_half, ones, vb) for vb in v_blocks]]

    def scores(t):
        q2 = (q_ref[t * qblk:(t + 1) * qblk, :].astype(F32) * qscale).astype(BF16)
        zero = jnp.zeros_like(q2)
        qq = jnp.concatenate([jnp.where(low_half, q2, zero), jnp.where(low_half, zero, q2)], axis=0)
        k_win = jnp.concatenate([kv_block(kp_ref, kc_ref, t + j) for j in range(n_kblk)], axis=0)
        return lax.dot_general(qq, k_win, (((1,), (1,)), ((), ())), preferred_element_type=F32)

    def finish(t, s2):
        variant = jnp.where(first, t + 1, 0) if t < n_past else 0
        outs = []
        for e in range(2):
            s = s2[e * qblk:(e + 1) * qblk, :] + bias_ref[variant, e]
            p = jnp.exp2(s - jnp.max(s, axis=1, keepdims=True)).astype(BF16)
            v_win = jnp.concatenate(v_heads[e][t:t + n_kblk], axis=0)
            outs.append(jnp.dot(p, v_win, preferred_element_type=F32))
        o = jnp.where(low_half, outs[0], outs[1])
        l = pltpu.roll(jnp.where(low_half, outs[1], outs[0]), dh, 1)
        o_ref[t * qblk:(t + 1) * qblk, :] = (o / l).astype(o_ref.dtype)

    s_next = scores(0)
    for t in range(n_sub):
        s_cur = s_next
        if t + 1 < n_sub:
            s_next = scores(t + 1)
        finish(t, s_cur)


def _attention(qg, kv, tables, *, seq, heads, dh):
    t = qg.shape[0]
    inner = heads * dh
    assert 2 * dh == LANES and heads % 2 == 0
    pairs = heads // 2
    qblk = ATTN_QBLK
    past = ATTN_PAST_CHUNKS * ATTN_CHUNK
    rows = min(ATTN_STEP_ROWS, seq)
    assert seq % rows == 0 and rows % past == 0 and past % qblk == 0 and qblk % ATTN_CHUNK == 0
    steps_per_seq = seq // rows
    ratio = rows // past
    v_off = inner // LANES
    n_var = tables.shape[0]
    width = tables.shape[3]
    prev_idx = lambda i: jnp.maximum(i * ratio - 1, 0)
    kernel = functools.partial(_attn_kernel, qblk=qblk, past=past, steps_per_seq=steps_per_seq, dh=dh)
    return pl.pallas_call(
        kernel,
        grid=(pairs, t // rows),
        in_specs=[pl.BlockSpec((rows, LANES), lambda p, i: (i, p)),
                  pl.BlockSpec((past, LANES), lambda p, i: (prev_idx(i), p)),
                  pl.BlockSpec((rows, LANES), lambda p, i: (i, p)),
                  pl.BlockSpec((past, LANES), lambda p, i: (prev_idx(i), v_off + p)),
                  pl.BlockSpec((rows, LANES), lambda p, i: (i, v_off + p)),
                  pl.BlockSpec((n_var, 2, qblk, width), lambda p, i: (0, p, 0, 0))],
        out_specs=pl.BlockSpec((rows, LANES), lambda p, i: (i, p)),
        out_shape=jax.ShapeDtypeStruct((t, inner), BF16),
        compiler_params=_params("parallel", "arbitrary"),
        name="chunk_attention",
    )(qg, kv, kv, kv, kv, tables)


def _attn_out_kernel(o_ref, g_ref, x_ref, w_ref, lg_ref, lb_ref, y_ref, *, alpha, sub):
    for r0 in range(0, o_ref.shape[0], sub):
        rows = slice(r0, r0 + sub)
        og = o_ref[rows, :] * _silu(g_ref[rows, :])
        y = jnp.dot(og, w_ref[...], preferred_element_type=F32)
        y_ref[rows, :] = _layer_norm(alpha * x_ref[rows, :] + y, lg_ref[...], lb_ref[...])


def _attn_out(o, qg, x, w_out, ln_g, ln_b, *, alpha, tm):
    t, inner = o.shape
    d = x.shape[1]
    tm = min(tm, t)
    vec = pl.BlockSpec((1, d), lambda i: (0, 0))
    return pl.pallas_call(
        functools.partial(_attn_out_kernel, alpha=alpha, sub=min(OUT_SUB_ROWS, tm)),
        grid=(t // tm,),
        in_specs=[pl.BlockSpec((tm, inner), lambda i: (i, 0)),
                  pl.BlockSpec((tm, inner), lambda i: (i, 1)),
                  pl.BlockSpec((tm, d), lambda i: (i, 0)),
                  pl.BlockSpec((inner, d), lambda i: (0, 0)),
                  vec, vec],
        out_specs=pl.BlockSpec((tm, d), lambda i: (i, 0)),
        out_shape=jax.ShapeDtypeStruct((t, d), F32),
        compiler_params=_params("parallel"),
        name="attn_out_ln",
    )(o, qg, x, w_out, ln_g.reshape(1, d), ln_b.reshape(1, d))


def _attn_layer(x, kv, p, *, seq, alpha):
    heads = p["rel_bias"].shape[0]
    inner = p["w_out"].shape[0]
    dh = inner // heads
    qg = _matmul(x, p["w_in"], bm=1024, bn=2048, out_dtype=BF16, name="attn_in_proj")
    tables = _bias_tables(p["rel_bias"], qblk=ATTN_QBLK, past=ATTN_PAST_CHUNKS * ATTN_CHUNK)
    o = _attention(qg, kv, tables, seq=seq, heads=heads, dh=dh)
    return _attn_out(o, qg, x, p["w_out"], p["ln_g"], p["ln_b"], alpha=alpha, tm=1024)


def kernel(x, a_w_in, a_b_gate, a_conv_w, a_conv_b, a_w_q, a_w_k, a_w_v, a_gn_w, a_skip, a_w_out,
           a_ln_g, a_ln_b, kv_w, b_w_in, b_rel_bias, b_w_out, b_ln_g, b_ln_b):
    batch, seq, d = x.shape
    n_a, n_b = a_w_in.shape[0], b_w_in.shape[0]
    alpha = (2.0 * (n_a + n_b)) ** 0.25
    heads = a_w_q.shape[1]
    inner = a_conv_w.shape[2]
    n_gate = 2 * heads
    h = x.reshape(batch * seq, d)
    for l in range(n_a):
        w_gate = jnp.pad(a_w_in[l][:, 3 * inner:], ((0, 0), (0, GATE_PAD - n_gate)))
        p = dict(w_in=a_w_in[l].astype(BF16), w_gate=w_gate.astype(BF16),
                 b_gate=jnp.pad(a_b_gate[l], (0, GATE_PAD - n_gate)),
                 conv_w=a_conv_w[l], conv_b=a_conv_b[l],
                 w_q=a_w_q[l].astype(BF16), w_v=a_w_v[l].astype(BF16),
                 w_k=(a_w_k[l] * float(a_w_k.shape[2]) ** -0.5).astype(BF16),
                 gn_w=a_gn_w[l], skip=a_skip[l], w_out=a_w_out[l].astype(BF16),
                 ln_g=a_ln_g[l], ln_b=a_ln_b[l])
        h = _mlstm_layer(h, p, batch=batch, seq=seq, alpha=alpha)
    kv = _matmul(h, kv_w.astype(BF16), bm=1024, bn=2048, out_dtype=BF16, name="shared_kv_proj")
    for l in range(n_b):
        p = dict(w_in=b_w_in[l].astype(BF16), rel_bias=b_rel_bias[l], w_out=b_w_out[l].astype(BF16),
                 ln_g=b_ln_g[l], ln_b=b_ln_b[l])
        h = _attn_layer(h, kv, p, seq=seq, alpha=alpha)
    return h.reshape(batch, seq, d)
```

```python
import functools

import jax
import jax.numpy as jnp
from jax import lax
from jax.experimental import pallas as pl
from jax.experimental.pallas import tpu as pltpu

F32 = jnp.float32
BF16 = jnp.bfloat16

LOG2E = 1.4426950408889634
LN_EPS = 1e-5
GN_EPS = 1e-6
ATTN_CHUNK = 64
ATTN_PAST_CHUNKS = 8
LANES = 128
GATE_PAD = 128
VMEM_LIMIT = 48 * 1024 * 1024

CONV_SUB_ROWS = 128
OUT_SUB_ROWS = 512
MLSTM_CHUNK = 256
MLSTM_CHUNKS_PER_STEP = 2
ATTN_QBLK = 256
ATTN_STEP_ROWS = 4096


def _params(*sem):
    return pltpu.CompilerParams(dimension_semantics=sem, vmem_limit_bytes=VMEM_LIMIT)


def _sigmoid(x):
    return 0.5 * jnp.tanh(0.5 * x) + 0.5


def _silu(x):
    h = 0.5 * x
    return h * jnp.tanh(h) + h


def _log_sigmoid(x):
    return jnp.minimum(x, 0.0) - jnp.log1p(jnp.exp(-jnp.abs(x)))


def _split3(x):
    hi = x.astype(BF16)
    r1 = x - hi.astype(F32)
    mid = r1.astype(BF16)
    lo = (r1 - mid.astype(F32)).astype(BF16)
    return hi, mid, lo


def _layer_norm(r, g, b):
    mu = jnp.mean(r, axis=-1, keepdims=True)
    d = r - mu
    var = jnp.mean(d * d, axis=-1, keepdims=True)
    return d * lax.rsqrt(var + LN_EPS) * g + b


def _mm_kernel(x_ref, w_ref, o_ref):
    acc = jnp.dot(x_ref[...].astype(BF16), w_ref[...], preferred_element_type=F32)
    o_ref[...] = acc.astype(o_ref.dtype)


def _mm_gate_kernel(x_ref, w_ref, wg_ref, bg_ref, o_ref, g_ref):
    x = x_ref[...].astype(BF16)
    o_ref[...] = jnp.dot(x, w_ref[...], preferred_element_type=F32).astype(o_ref.dtype)

    @pl.when(pl.program_id(1) == 0)
    def _():
        g_ref[...] = jnp.dot(x, wg_ref[...], preferred_element_type=F32) + bg_ref[...]


def _kv_kernel(x_ref, wk_ref, wvt_ref, k_ref, vt_ref):
    x = x_ref[...].astype(BF16)
    k_ref[...] = jnp.dot(x, wk_ref[...], preferred_element_type=F32).astype(k_ref.dtype)
    vt_ref[...] = lax.dot_general(wvt_ref[...], x, (((1,), (1,)), ((), ())),
                                  preferred_element_type=F32).astype(vt_ref.dtype)


def _kv_proj(x, w_k, w_vt, *, bm):
    m, kdim = x.shape
    n = w_k.shape[1]
    bm = min(bm, m)
    assert m % bm == 0
    return pl.pallas_call(
        _kv_kernel,
        grid=(m // bm,),
        in_specs=[pl.BlockSpec((bm, kdim), lambda i: (i, 0)),
                  pl.BlockSpec((kdim, n), lambda i: (0, 0)),
                  pl.BlockSpec((n, kdim), lambda i: (0, 0))],
        out_specs=[pl.BlockSpec((bm, n), lambda i: (i, 0)),
                   pl.BlockSpec((n, bm), lambda i: (0, i))],
        out_shape=[jax.ShapeDtypeStruct((m, n), BF16), jax.ShapeDtypeStruct((n, m), BF16)],
        compiler_params=_params("parallel"),
        name="shared_kv_proj",
    )(x, w_k, w_vt)


def _col_block(n, bn):
    bn = min(bn, n)
    while n % bn:
        bn -= LANES
    assert bn > 0
    return bn


def _matmul(x, w, *, bm, bn, out_dtype, name):
    m, k = x.shape
    n = w.shape[1]
    bm, bn = min(bm, m), _col_block(n, bn)
    assert m % bm == 0
    return pl.pallas_call(
        _mm_kernel,
        grid=(m // bm, n // bn),
        in_specs=[pl.BlockSpec((bm, k), lambda i, j: (i, 0)),
                  pl.BlockSpec((k, bn), lambda i, j: (0, j))],
        out_specs=pl.BlockSpec((bm, bn), lambda i, j: (i, j)),
        out_shape=jax.ShapeDtypeStruct((m, n), out_dtype),
        compiler_params=_params("parallel", "arbitrary"),
        name=name,
    )(x, w)


def _in_proj_with_gates(x, w, w_gate, b_gate, *, n, bm, bn, name):
    m, k = x.shape
    bm, bn = min(bm, m), _col_block(n, bn)
    ng = w_gate.shape[1]
    assert m % bm == 0
    return pl.pallas_call(
        _mm_gate_kernel,
        grid=(m // bm, n // bn),
        in_specs=[pl.BlockSpec((bm, k), lambda i, j: (i, 0)),
                  pl.BlockSpec((k, bn), lambda i, j: (0, j)),
                  pl.BlockSpec((k, ng), lambda i, j: (0, 0)),
                  pl.BlockSpec((1, ng), lambda i, j: (0, 0))],
        out_specs=[pl.BlockSpec((bm, bn), lambda i, j: (i, j)),
                   pl.BlockSpec((bm, ng), lambda i, j: (i, 0))],
        out_shape=[jax.ShapeDtypeStruct((m, n), BF16), jax.ShapeDtypeStruct((m, ng), F32)],
        compiler_params=_params("parallel", "arbitrary"),
        name=name,
    )(x, w, w_gate, b_gate.reshape(1, ng))


def _conv_qkv_kernel(xm_ref, cw_ref, cb_ref, wq_ref, wk_ref, wv_ref,
                     xc_ref, q_ref, k_ref, v_ref, xbuf, *, tm, sub, tiles_per_seq, taps):
    i = pl.program_id(1)
    halo = xbuf.shape[0]

    @pl.when(i % tiles_per_seq == 0)
    def _():
        xbuf[...] = jnp.zeros(xbuf.shape, F32)

    def conv(r0, prev):
        xm = xm_ref[r0:r0 + sub, :]
        x = xm.astype(F32)
        xe = jnp.concatenate([prev, x], axis=0)
        acc = cb_ref[...] + x * cw_ref[taps - 1:taps, :]
        for shift in range(1, taps):
            xs = pltpu.roll(xe, shift, 0)[halo:halo + sub, :]
            acc = acc + xs * cw_ref[taps - 1 - shift:taps - shift, :]
        return xm, _silu(acc.astype(BF16)), x[sub - halo:sub, :]

    def project(r0, xm, xc):
        rows = slice(r0, r0 + sub)
        xc_ref[rows, :] = xc
        q_ref[rows, :] = jnp.dot(xc, wq_ref[...], preferred_element_type=F32).astype(BF16)
        k_ref[rows, :] = jnp.dot(xc, wk_ref[...], preferred_element_type=F32).astype(BF16)
        v_ref[rows, :] = jnp.dot(xm, wv_ref[...], preferred_element_type=F32).astype(BF16)

    xm, xc, prev = conv(0, xbuf[...])
    for r0 in range(0, tm, sub):
        cur = (xm, xc)
        if r0 + sub < tm:
            xm, xc, prev = conv(r0 + sub, prev)
        project(r0, *cur)
    xbuf[...] = prev


def _conv_qkv(u, conv_w, conv_b, w_q, w_k, w_v, *, seq, tm):
    t = u.shape[0]
    heads, dh, _ = w_q.shape
    inner = heads * dh
    taps = conv_w.shape[0]
    tm = min(tm, seq)
    assert seq % tm == 0 and taps - 1 <= 8
    out = jax.ShapeDtypeStruct((t, inner), BF16)
    row_blk = pl.BlockSpec((tm, dh), lambda h, i: (i, h))
    w_blk = pl.BlockSpec((None, dh, dh), lambda h, i: (h, 0, 0))
    sub = min(CONV_SUB_ROWS, tm)
    assert tm % sub == 0
    kernel = functools.partial(_conv_qkv_kernel, tm=tm, sub=sub, tiles_per_seq=seq // tm, taps=taps)
    return pl.pallas_call(
        kernel,
        grid=(heads, t // tm),
        in_specs=[row_blk,
                  pl.BlockSpec((taps, dh), lambda h, i: (0, h)),
                  pl.BlockSpec((1, dh), lambda h, i: (0, h)),
                  w_blk, w_blk, w_blk],
        out_specs=[row_blk, row_blk, row_blk, row_blk],
        out_shape=[out, out, out, out],
        scratch_shapes=[pltpu.VMEM((8, dh), F32)],
        compiler_params=_params("arbitrary", "arbitrary"),
        name="mlstm_conv_qkv",
    )(u, conv_w, conv_b.reshape(1, inner), w_q, w_k, w_v)


def _mlstm_kernel(q_ref, k_ref, v_ref, g_ref, h_ref, c_scr, n_scr, m_scr, *, chunk, heads, dh):
    steps = g_ref.shape[0]

    @pl.when(pl.program_id(1) == 0)
    def _():
        c_scr[...] = jnp.zeros(c_scr.shape, F32)
        n_scr[...] = jnp.zeros(n_scr.shape, F32)
        m_scr[...] = jnp.zeros(m_scr.shape, F32)

    t_id = lax.broadcasted_iota(jnp.int32, (chunk, chunk), 0)
    s_id = lax.broadcasted_iota(jnp.int32, (chunk, chunk), 1)
    causal = s_id <= t_id
    triu = jnp.where(t_id <= s_id, 1.0, 0.0).astype(BF16)

    i_rows, lf_rows, bcum_rows = [], [], []
    for cc in range(steps):
        grow = g_ref[cc]
        i_rows.append(grow[:heads, :] * LOG2E)
        lf = _log_sigmoid(grow[heads:, :]) * LOG2E
        lf_rows.append(lf)
        br = jnp.dot(jnp.concatenate(_split3(lf), axis=0), triu, preferred_element_type=F32)
        bcum_rows.append(br[:heads] + br[heads:2 * heads] + br[2 * heads:])

    def state_products(cc, h):
        rows, sl = slice(cc * chunk, (cc + 1) * chunk), slice(h * dh, (h + 1) * dh)
        q = q_ref[rows, sl]
        s = lax.dot_general(q, k_ref[rows, sl], (((1,), (1,)), ((), ())), preferred_element_type=F32)
        qc = jnp.dot(q, c_scr[h].astype(BF16), preferred_element_type=F32)
        qn = lax.dot_general(q, n_scr[h].astype(BF16), (((1,), (1,)), ((), ())),
                             preferred_element_type=F32)
        return s, qc, qn

    def finish(cc, h, s, qc, qn):
        rows, sl = slice(cc * chunk, (cc + 1) * chunk), slice(h * dh, (h + 1) * dh)
        lf_row = lf_rows[cc][h:h + 1, :]
        bcum_col = jnp.sum(jnp.where(causal, lf_row, 0.0), axis=1, keepdims=True)
        m_prev = m_scr[h, 0:1, 0:1]
        b_last = bcum_col[chunk - 1:chunk, :]
        r_row = i_rows[cc][h:h + 1, :] - bcum_rows[cc][h:h + 1, :]
        dmat = jnp.where(causal, bcum_col + r_row, -jnp.inf)
        m_inter = bcum_col + m_prev
        m_t = jnp.maximum(m_inter, jnp.max(dmat, axis=1, keepdims=True))
        w_inter = jnp.exp2(m_inter - m_t)
        sw = s * jnp.exp2(dmat - m_t)
        v = v_ref[rows, sl]
        num = jnp.dot(sw.astype(BF16), v, preferred_element_type=F32) + w_inter * qc
        den = jnp.sum(sw, axis=1, keepdims=True) + w_inter * qn
        den = jnp.maximum(jnp.abs(den), jnp.exp2(-m_t))
        mu = jnp.mean(num, axis=1, keepdims=True)
        d = num - mu
        var = jnp.mean(d * d, axis=1, keepdims=True)
        scale = lax.rsqrt(var + GN_EPS * (den * den))
        h_ref[rows, sl] = (d * jnp.concatenate([scale] * (dh // LANES), axis=1)).astype(h_ref.dtype)

        g_row = b_last + r_row
        m_new = jnp.maximum(b_last + m_prev, jnp.max(g_row, axis=1, keepdims=True))
        decay = jnp.exp2(b_last + m_prev - m_new)
        wg_row = jnp.exp2(g_row - m_new).astype(BF16)
        k = k_ref[rows, sl]
        kt_w = jnp.transpose(k) * wg_row
        c_scr[h] = decay * c_scr[h] + jnp.dot(kt_w, v, preferred_element_type=F32)
        n_scr[h] = decay * n_scr[h] + jnp.dot(jnp.broadcast_to(wg_row, (LANES, chunk)), k,
                                              preferred_element_type=F32)
        m_scr[h] = jnp.broadcast_to(m_new, m_scr.shape[1:])

    items = [(cc, h) for cc in range(steps) for h in range(heads)]
    nxt = state_products(*items[0])
    for idx, item in enumerate(items):
        cur = nxt
        if idx + 1 < len(items):
            nxt = state_products(*items[idx + 1])
        finish(*item, *cur)


def _mlstm(q, k, v, gates, *, batch, seq, heads, chunk):
    t, inner = q.shape
    dh = inner // heads
    chunk = min(chunk, seq)
    assert seq % chunk == 0
    nc = seq // chunk
    g_rows = gates[:, :2 * heads].reshape(batch * nc, chunk, 2 * heads).transpose(0, 2, 1)
    cps = MLSTM_CHUNKS_PER_STEP if nc % MLSTM_CHUNKS_PER_STEP == 0 else 1
    ns = nc // cps
    blk = pl.BlockSpec((cps * chunk, inner), lambda b, c: (b * ns + c, 0))
    kernel = functools.partial(_mlstm_kernel, chunk=chunk, heads=heads, dh=dh)
    return pl.pallas_call(
        kernel,
        grid=(batch, ns),
        in_specs=[blk, blk, blk,
                  pl.BlockSpec((cps, 2 * heads, chunk), lambda b, c: (b * ns + c, 0, 0))],
        out_specs=blk,
        out_shape=jax.ShapeDtypeStruct((t, inner), BF16),
        scratch_shapes=[pltpu.VMEM((heads, dh, dh), F32),
                        pltpu.VMEM((heads, LANES, dh), F32),
                        pltpu.VMEM((heads, 8, LANES), F32)],
        compiler_params=_params("arbitrary", "arbitrary"),
        name="mlstm_scan",
    )(q, k, v, g_rows)


def _mlstm_out_kernel(hn_ref, o_ref, z_ref, xc_ref, x_ref, gnw_ref, skip_ref, w_ref, g_ref, b_ref,
                      y_ref, *, alpha, sub):
    gnw_half = (0.5 * gnw_ref[...]).astype(BF16)
    skip = skip_ref[...].astype(BF16)
    for r0 in range(0, hn_ref.shape[0], sub):
        rows = slice(r0, r0 + sub)
        a = hn_ref[rows, :] * gnw_half
        hh = a + a * jnp.tanh(o_ref[rows, :] * 0.5) + skip * xc_ref[rows, :]
        hz = hh * _silu(z_ref[rows, :])
        y = jnp.dot(hz, w_ref[...], preferred_element_type=F32)
        y_ref[rows, :] = _layer_norm(alpha * x_ref[rows, :] + y, g_ref[...], b_ref[...])


def _mlstm_out(hn, u, xc, x, gn_w, skip, w_out, ln_g, ln_b, *, alpha, tm):
    t, inner = hn.shape
    d = x.shape[1]
    tm = min(tm, t)
    row = lambda cb: pl.BlockSpec((tm, inner), lambda i, cb=cb: (i, cb))
    vec = lambda n: pl.BlockSpec((1, n), lambda i: (0, 0))
    return pl.pallas_call(
        functools.partial(_mlstm_out_kernel, alpha=alpha, sub=min(OUT_SUB_ROWS, tm)),
        grid=(t // tm,),
        in_specs=[row(0), row(2), row(1), row(0),
                  pl.BlockSpec((tm, d), lambda i: (i, 0)),
                  vec(inner), vec(inner),
                  pl.BlockSpec((inner, d), lambda i: (0, 0)),
                  vec(d), vec(d)],
        out_specs=pl.BlockSpec((tm, d), lambda i: (i, 0)),
        out_shape=jax.ShapeDtypeStruct((t, d), F32),
        compiler_params=_params("parallel"),
        name="mlstm_out_ln",
    )(hn, u, u, xc, x, gn_w.reshape(1, inner), skip.reshape(1, inner), w_out,
      ln_g.reshape(1, d), ln_b.reshape(1, d))


def _mlstm_layer(x, p, *, batch, seq, alpha):
    heads, dh, _ = p["w_q"].shape
    inner = heads * dh
    u, gates = _in_proj_with_gates(x, p["w_in"], p["w_gate"], p["b_gate"], n=3 * inner,
                                   bm=1024, bn=2048, name="mlstm_in_proj")
    xc, q, k, v = _conv_qkv(u, p["conv_w"], p["conv_b"], p["w_q"], p["w_k"], p["w_v"],
                            seq=seq, tm=2048)
    hn = _mlstm(q, k, v, gates, batch=batch, seq=seq, heads=heads, chunk=MLSTM_CHUNK)
    return _mlstm_out(hn, u, xc, x, p["gn_w"], p["skip"], p["w_out"], p["ln_g"], p["ln_b"],
                      alpha=alpha, tm=512)


def _bias_table_kernel(r_ref, o_ref, *, qblk, width, past):
    rolled = pltpu.roll(jnp.broadcast_to(r_ref[0] * LOG2E, (qblk, r_ref.shape[2])), 0, 1,
                        stride=1, stride_axis=0)
    bias = rolled[:, :width]
    cq = lax.broadcasted_iota(jnp.int32, (qblk, width), 0) // ATTN_CHUNK
    col = lax.broadcasted_iota(jnp.int32, (qblk, width), 1)
    ck = col // ATTN_CHUNK
    band = (ck >= cq) & (ck <= cq + ATTN_PAST_CHUNKS)
    for variant in range(o_ref.shape[0]):
        first_valid = 0 if variant == 0 else past - (variant - 1) * qblk
        o_ref[variant, 0] = jnp.transpose(jnp.where(band & (col >= first_valid), bias, -jnp.inf))


def _bias_tables(rel_bias, *, qblk, past):
    heads, nrel = rel_bias.shape
    clip = (nrel - 1) // 2
    width = past + qblk
    rlen = 1 << (width + qblk - 1).bit_length()
    assert past >= clip and width - past >= 0
    far_past = jnp.broadcast_to(rel_bias[:, nrel - 1:], (heads, past - clip))
    future = jnp.broadcast_to(rel_bias[:, :1], (heads, width - (past + clip)))
    wrapped = jnp.broadcast_to(rel_bias[:, nrel - 1:], (heads, rlen - width - 1))
    r = jnp.concatenate([far_past, rel_bias[:, ::-1], future, wrapped], axis=1)
    assert r.shape[1] == rlen
    n_var = past // qblk + 1
    return pl.pallas_call(
        functools.partial(_bias_table_kernel, qblk=qblk, width=width, past=past),
        grid=(heads,),
        in_specs=[pl.BlockSpec((1, 1, rlen), lambda h: (h, 0, 0))],
        out_specs=pl.BlockSpec((n_var, 1, width, qblk), lambda h: (0, h, 0, 0)),
        out_shape=jax.ShapeDtypeStruct((n_var, heads, width, qblk), F32),
        compiler_params=_params("parallel"),
        name="attn_bias_table",
    )(r.reshape(heads, 1, rlen).astype(F32))


def _attn_kernel(q_ref, kp_ref, kc_ref, vtp_ref, vtc_ref, bias_ref, o_ref,
                 *, qblk, past, steps_per_seq, dh):
    first = (pl.program_id(1) % steps_per_seq) == 0
    rows = q_ref.shape[0]
    n_past = past // qblk
    n_kblk = n_past + 1
    n_sub = rows // qblk
    width = n_kblk * qblk
    low_lanes = lax.broadcasted_iota(jnp.int32, (qblk, LANES), 1) < dh
    low_rows = lax.broadcasted_iota(jnp.int32, (LANES, qblk), 0) < dh
    qscale = LOG2E * float(dh) ** -0.5

    def k_block(j):
        if j < n_past:
            return kp_ref[j * qblk:(j + 1) * qblk, :]
        return kc_ref[(j - n_past) * qblk:(j - n_past + 1) * qblk, :]

    vt = jnp.concatenate([vtp_ref[...], vtc_ref[...]], axis=1)
    low_rows_all = lax.broadcasted_iota(jnp.int32, vt.shape, 0) < dh
    one = jnp.ones_like(vt)
    vt_heads = [jnp.where(low_rows_all, vt, one), jnp.where(low_rows_all, one, vt)]

    def scores(t):
        q2 = (q_ref[t * qblk:(t + 1) * qblk, :].astype(F32) * qscale).astype(BF16)
        zero = jnp.zeros_like(q2)
        qq = jnp.concatenate([jnp.where(low_lanes, q2, zero), jnp.where(low_lanes, zero, q2)], axis=0)
        k_win = jnp.concatenate([k_block(t + j) for j in range(n_kblk)], axis=0)
        return lax.dot_general(k_win, qq, (((1,), (1,)), ((), ())), preferred_element_type=F32)

    def finish(t, s2t):
        variant = jnp.where(first, t + 1, 0) if t < n_past else 0
        outs = []
        for e in range(2):
            s = s2t[:, e * qblk:(e + 1) * qblk] + bias_ref[variant, e]
            p = jnp.exp2(s - jnp.max(s, axis=0, keepdims=True)).astype(BF16)
            vt_win = vt_heads[e][:, t * qblk:t * qblk + width]
            outs.append(jnp.dot(vt_win, p, preferred_element_type=F32))
        o = jnp.where(low_rows, outs[0], outs[1])
        l = jnp.where(low_rows, outs[1], outs[0])
        l = jnp.concatenate([l[dh:, :], l[:dh, :]], axis=0)
        o_ref[t * qblk:(t + 1) * qblk, :] = jnp.transpose(o / l).astype(o_ref.dtype)

    s_next = scores(0)
    for t in range(n_sub):
        s_cur = s_next
        if t + 1 < n_sub:
            s_next = scores(t + 1)
        finish(t, s_cur)


def _attention(qg, k, vt, tables, *, seq, heads, dh):
    t = qg.shape[0]
    inner = heads * dh
    assert 2 * dh == LANES and heads % 2 == 0
    pairs = heads // 2
    qblk = ATTN_QBLK
    past = ATTN_PAST_CHUNKS * ATTN_CHUNK
    rows = min(ATTN_STEP_ROWS, seq)
    assert seq % rows == 0 and rows % past == 0 and past % qblk == 0 and qblk % ATTN_CHUNK == 0
    steps_per_seq = seq // rows
    ratio = rows // past
    n_var = tables.shape[0]
    width = tables.shape[2]
    prev_idx = lambda i: jnp.maximum(i * ratio - 1, 0)
    kernel = functools.partial(_attn_kernel, qblk=qblk, past=past, steps_per_seq=steps_per_seq, dh=dh)
    return pl.pallas_call(
        kernel,
        grid=(pairs, t // rows),
        in_specs=[pl.BlockSpec((rows, LANES), lambda p, i: (i, p)),
                  pl.BlockSpec((past, LANES), lambda p, i: (prev_idx(i), p)),
                  pl.BlockSpec((rows, LANES), lambda p, i: (i, p)),
                  pl.BlockSpec((LANES, past), lambda p, i: (p, prev_idx(i))),
                  pl.BlockSpec((LANES, rows), lambda p, i: (p, i)),
                  pl.BlockSpec((n_var, 2, width, qblk), lambda p, i: (0, p, 0, 0))],
        out_specs=pl.BlockSpec((rows, LANES), lambda p, i: (i, p)),
        out_shape=jax.ShapeDtypeStruct((t, inner), BF16),
        compiler_params=_params("parallel", "arbitrary"),
        name="chunk_attention",
    )(qg, k, k, vt, vt, tables)


def _attn_out_kernel(o_ref, g_ref, x_ref, w_ref, lg_ref, lb_ref, y_ref, *, alpha, sub):
    for r0 in range(0, o_ref.shape[0], sub):
        rows = slice(r0, r0 + sub)
        og = o_ref[rows, :] * _silu(g_ref[rows, :])
        y = jnp.dot(og, w_ref[...], preferred_element_type=F32)
        y_ref[rows, :] = _layer_norm(alpha * x_ref[rows, :] + y, lg_ref[...], lb_ref[...])


def _attn_out(o, qg, x, w_out, ln_g, ln_b, *, alpha, tm):
    t, inner = o.shape
    d = x.shape[1]
    tm = min(tm, t)
    vec = pl.BlockSpec((1, d), lambda i: (0, 0))
    return pl.pallas_call(
        functools.partial(_attn_out_kernel, alpha=alpha, sub=min(OUT_SUB_ROWS, tm)),
        grid=(t // tm,),
        in_specs=[pl.BlockSpec((tm, inner), lambda i: (i, 0)),
                  pl.BlockSpec((tm, inner), lambda i: (i, 1)),
                  pl.BlockSpec((tm, d), lambda i: (i, 0)),
                  pl.BlockSpec((inner, d), lambda i: (0, 0)),
                  vec, vec],
        out_specs=pl.BlockSpec((tm, d), lambda i: (i, 0)),
        out_shape=jax.ShapeDtypeStruct((t, d), F32),
        compiler_params=_params("parallel"),
        name="attn_out_ln",
    )(o, qg, x, w_out, ln_g.reshape(1, d), ln_b.reshape(1, d))


def _attn_layer(x, k, vt, p, *, seq, alpha):
    heads = p["rel_bias"].shape[0]
    inner = p["w_out"].shape[0]
    dh = inner // heads
    qg = _matmul(x, p["w_in"], bm=1024, bn=2048, out_dtype=BF16, name="attn_in_proj")
    tables = _bias_tables(p["rel_bias"], qblk=ATTN_QBLK, past=ATTN_PAST_CHUNKS * ATTN_CHUNK)
    o = _attention(qg, k, vt, tables, seq=seq, heads=heads, dh=dh)
    return _attn_out(o, qg, x, p["w_out"], p["ln_g"], p["ln_b"], alpha=alpha, tm=1024)


def kernel(x, a_w_in, a_b_gate, a_conv_w, a_conv_b, a_w_q, a_w_k, a_w_v, a_gn_w, a_skip, a_w_out,
           a_ln_g, a_ln_b, kv_w, b_w_in, b_rel_bias, b_w_out, b_ln_g, b_ln_b):
    batch, seq, d = x.shape
    n_a, n_b = a_w_in.shape[0], b_w_in.shape[0]
    alpha = (2.0 * (n_a + n_b)) ** 0.25
    heads = a_w_q.shape[1]
    inner = a_conv_w.shape[2]
    n_gate = 2 * heads
    h = x.reshape(batch * seq, d)
    for l in range(n_a):
        w_gate = jnp.pad(a_w_in[l][:, 3 * inner:], ((0, 0), (0, GATE_PAD - n_gate)))
        p = dict(w_in=a_w_in[l].astype(BF16), w_gate=w_gate.astype(BF16),
                 b_gate=jnp.pad(a_b_gate[l], (0, GATE_PAD - n_gate)),
                 conv_w=a_conv_w[l], conv_b=a_conv_b[l],
                 w_q=a_w_q[l].astype(BF16), w_v=a_w_v[l].astype(BF16),
                 w_k=(a_w_k[l] * float(a_w_k.shape[2]) ** -0.5).astype(BF16),
                 gn_w=a_gn_w[l], skip=a_skip[l], w_out=a_w_out[l].astype(BF16),
                 ln_g=a_ln_g[l], ln_b=a_ln_b[l])
        h = _mlstm_layer(h, p, batch=batch, seq=seq, alpha=alpha)
    b_inner = kv_w.shape[1] // 2
    k, vt = _kv_proj(h, kv_w[:, :b_inner].astype(BF16), kv_w[:, b_inner:].T.astype(BF16), bm=1024)
    for l in range(n_b):
        p = dict(w_in=b_w_in[l].astype(BF16), rel_bias=b_rel_bias[l], w_out=b_w_out[l].astype(BF16),
                 ln_g=b_ln_g[l], ln_b=b_ln_b[l])
        h = _attn_layer(h, k, vt, p, seq=seq, alpha=alpha)
    return h.reshape(batch, seq, d)
```
